```python
import jax, jax.numpy as jnp
from jax import lax
import numpy as np

D_MODEL = 1024
BATCH = 8
SEQ = 4096
DEPTH = 1

N_HEADS = 8
HEAD_DIM = 64
ATTN_WIDTH = N_HEADS * HEAD_DIM
KV_RANK = 128
IDX_HEADS = 8
IDX_DIM = 64
TOPK_MAX = 256
Q_BLOCK = 128
CHUNK = 128
GMLP_GROUPS = 4
GMLP_GROUP_DIM = 128
GMLP_WIDTH = GMLP_GROUPS * GMLP_GROUP_DIM
NUM_BUCKETS = 32
MAX_DISTANCE = 128
N_GROUPS = 4
EXPERTS_PER_GROUP = 8
N_EXPERTS = N_GROUPS * EXPERTS_PER_GROUP
TOP_K_INNER = 2
D_FF_EXPERT = 256
EPS = 1e-6
IN_SIZES = (ATTN_WIDTH, KV_RANK, IDX_HEADS * IDX_DIM, IDX_DIM, IDX_HEADS, GMLP_WIDTH, GMLP_WIDTH, D_MODEL, D_MODEL)
IN_COLS = ATTN_WIDTH + KV_RANK + IDX_HEADS * IDX_DIM + IDX_DIM + IDX_HEADS + 2 * GMLP_WIDTH + 2 * D_MODEL

kernel_name = "hybrid_dsa_gmlp_hmoe"


def rms_norm(x, g):
    xf = x.astype(jnp.float32)
    y = xf * lax.rsqrt(jnp.mean(xf * xf, axis=-1, keepdims=True) + EPS)
    return y.astype(x.dtype) * g


def layer_norm(x, g, b):
    xf = x.astype(jnp.float32)
    mu = jnp.mean(xf, axis=-1, keepdims=True)
    var = jnp.mean(jnp.square(xf - mu), axis=-1, keepdims=True)
    return ((xf - mu) * lax.rsqrt(var + EPS)).astype(x.dtype) * g + b


def t5_bucket(n):
    max_exact = NUM_BUCKETS // 2
    nf = jnp.maximum(n, 1).astype(jnp.float32)
    large = max_exact + (jnp.log(nf / max_exact) / np.float32(np.log(MAX_DISTANCE / max_exact)) * (NUM_BUCKETS - max_exact)).astype(jnp.int32)
    large = jnp.minimum(large, NUM_BUCKETS - 1)
    return jnp.where(n < max_exact, n, large)


def sparse_attention(q_lat, q_idx, w_idx, c_kv, k_idx, rel_bias):
    B, T = c_kv.shape[:2]
    nb = T // Q_BLOCK
    topk = min(TOPK_MAX, T // 4)
    scale = HEAD_DIM ** -0.5
    s_pos = jnp.arange(T, dtype=jnp.int32)

    def to_blocks(a):
        return jnp.moveaxis(a.reshape((B, nb, Q_BLOCK) + a.shape[2:]), 1, 0)

    def block(args):
        blk, ql, qi, wi = args
        t_pos = blk * Q_BLOCK + jnp.arange(Q_BLOCK, dtype=jnp.int32)
        rel = jax.nn.relu(jnp.einsum('bqhd,bsd->bqhs', qi, k_idx))
        index = jnp.einsum('bqhs,bqh->bqs', rel, wi)
        causal = s_pos[None, :] <= t_pos[:, None]
        index = jnp.where(causal[None], index, -jnp.inf)
        _, sel = lax.top_k(index, topk)
        c_sel = jax.vmap(lambda c, i: c[i])(c_kv, sel)
        dist = t_pos[None, :, None] - sel
        valid = dist >= 0
        bias = rel_bias[t5_bucket(jnp.maximum(dist, 0))]
        scores = jnp.einsum('bqhr,bqkr->bqhk', ql, c_sel).astype(jnp.float32) * scale
        scores = scores + jnp.transpose(bias, (0, 1, 3, 2)).astype(jnp.float32)
        scores = jnp.where(valid[:, :, None, :], scores, -jnp.inf)
        p = jax.nn.softmax(scores, axis=-1).astype(c_sel.dtype)
        return jnp.einsum('bqhk,bqkr->bqhr', p, c_sel)

    out = lax.map(block, (jnp.arange(nb, dtype=jnp.int32), to_blocks(q_lat), to_blocks(q_idx), to_blocks(w_idx)))
    return jnp.moveaxis(out, 0, 1).reshape(B, T, N_HEADS, KV_RANK)


def chunked_gmlp(u, v, ln_g, ln_b, w_s, b_s):
    B, T = u.shape[:2]
    u = jax.nn.gelu(u)
    v = layer_norm(jax.nn.gelu(v), ln_g, ln_b)
    vc = v.reshape(B, T // CHUNK, CHUNK, GMLP_GROUPS, GMLP_GROUP_DIM)
    mask = jnp.tril(jnp.ones((CHUNK, CHUNK), dtype=bool))
    ws = jnp.where(mask[None], w_s, 0.0)
    s = jnp.einsum('gij,bcjge->bcige', ws, vc) + jnp.transpose(b_s)[None, None, :, :, None]
    return u * s.reshape(B, T, GMLP_WIDTH)


def hier_moe(h, rg_w, rg_b, re_w, re_b, w_gate, w_up, w_down):
    N = h.shape[0]
    gp = jax.nn.softmax((h @ rg_w + rg_b).astype(jnp.float32), axis=-1)
    g_w, g_idx = lax.top_k(gp, 1)
    el = (h @ re_w + re_b).reshape(N, N_GROUPS, EXPERTS_PER_GROUP)
    el = jnp.take_along_axis(el, g_idx[:, :, None], axis=1)[:, 0]
    ep = jax.nn.softmax(el.astype(jnp.float32), axis=-1)
    e_w, e_idx = lax.top_k(ep, TOP_K_INNER)
    e_w = g_w * e_w / jnp.sum(e_w, axis=-1, keepdims=True)
    eid = g_idx * EXPERTS_PER_GROUP + e_idx
    comb = jnp.sum(jax.nn.one_hot(eid, N_EXPERTS, dtype=jnp.float32) * e_w[..., None], axis=1).astype(h.dtype)
    y = jnp.zeros_like(h)
    for e in range(N_EXPERTS):
        he = jax.nn.silu(h @ w_gate[e]) * (h @ w_up[e])
        y = y + comb[:, e:e + 1] * (he @ w_down[e])
    return y


def setup_inputs(seed: int = 0) -> dict:
    key = jax.random.key(seed)
    ks = jax.random.split(key, 24)

    def nrm(k, shape, s):
        return jax.random.normal(k, shape, jnp.float32) * s

    return {
        "x": nrm(ks[0], (BATCH, SEQ, D_MODEL), 1.0),
        "w_in": nrm(ks[1], (DEPTH, D_MODEL, IN_COLS), D_MODEL ** -0.5),
        "kv_norm_g": 1.0 + nrm(ks[2], (DEPTH, KV_RANK), 0.01),
        "w_uk": nrm(ks[3], (DEPTH, KV_RANK, N_HEADS, HEAD_DIM), KV_RANK ** -0.5),
        "w_uv": nrm(ks[4], (DEPTH, KV_RANK, N_HEADS, HEAD_DIM), KV_RANK ** -0.5),
        "rel_bias": nrm(ks[5], (NUM_BUCKETS, N_HEADS), 0.5),
        "ln_v_g": 1.0 + nrm(ks[6], (DEPTH, GMLP_WIDTH), 0.01),
        "ln_v_b": nrm(ks[7], (DEPTH, GMLP_WIDTH), 0.01),
        "w_spatial": nrm(ks[8], (DEPTH, GMLP_GROUPS, CHUNK, CHUNK), CHUNK ** -0.5),
        "b_spatial": 1.0 + nrm(ks[9], (DEPTH, GMLP_GROUPS, CHUNK), 0.01),
        "w_proj_a": nrm(ks[10], (DEPTH, ATTN_WIDTH, D_MODEL), ATTN_WIDTH ** -0.5),
        "w_proj_b": nrm(ks[11], (DEPTH, GMLP_WIDTH, D_MODEL), GMLP_WIDTH ** -0.5),
        "w_out": nrm(ks[12], (DEPTH, D_MODEL, D_MODEL), D_MODEL ** -0.5),
        "norm1_g": 1.0 + nrm(ks[13], (DEPTH, D_MODEL), 0.01),
        "norm2_g": 1.0 + nrm(ks[14], (DEPTH, D_MODEL), 0.01),
        "router_group_w": nrm(ks[15], (DEPTH, D_MODEL, N_GROUPS), D_MODEL ** -0.5),
        "router_group_b": nrm(ks[16], (DEPTH, N_GROUPS), 0.01),
        "router_expert_w": nrm(ks[17], (DEPTH, D_MODEL, N_EXPERTS), D_MODEL ** -0.5),
        "router_expert_b": nrm(ks[18], (DEPTH, N_EXPERTS), 0.01),
        "w_gate": nrm(ks[19], (DEPTH, N_EXPERTS, D_MODEL, D_FF_EXPERT), D_MODEL ** -0.5),
        "w_up": nrm(ks[20], (DEPTH, N_EXPERTS, D_MODEL, D_FF_EXPERT), D_MODEL ** -0.5),
        "w_down": nrm(ks[21], (DEPTH, N_EXPERTS, D_FF_EXPERT, D_MODEL), D_FF_EXPERT ** -0.5),
        "final_norm_g": 1.0 + nrm(ks[22], (D_MODEL,), 0.01),
    }


def reference(x, w_in, kv_norm_g, w_uk, w_uv, rel_bias, ln_v_g, ln_v_b, w_spatial, b_spatial, w_proj_a, w_proj_b, w_out, norm1_g, norm2_g, router_group_w, router_group_b, router_expert_w, router_expert_b, w_gate, w_up, w_down, final_norm_g):
    B, T, D = x.shape
    splits = [int(c) for c in np.cumsum(IN_SIZES)[:-1]]
    idx_scale = (IDX_HEADS ** -0.5) * (IDX_DIM ** -0.5)
    for l in range(DEPTH):
        h = rms_norm(x, norm1_g[l])
        proj = h @ w_in[l]
        q, ckv, qi, ki, wi, u, v, ga, gb = jnp.split(proj, splits, axis=-1)
        q = q.reshape(B, T, N_HEADS, HEAD_DIM)
        c_kv = rms_norm(ckv, kv_norm_g[l])
        q_lat = jnp.einsum('bthd,rhd->bthr', q, w_uk[l])
        q_idx = qi.reshape(B, T, IDX_HEADS, IDX_DIM)
        w_idx = wi * idx_scale
        o_lat = sparse_attention(q_lat, q_idx, w_idx, c_kv, ki, rel_bias)
        o_a = jnp.einsum('bthr,rhd->bthd', o_lat, w_uv[l]).reshape(B, T, ATTN_WIDTH)
        y_a = o_a @ w_proj_a[l]
        y_b = chunked_gmlp(u, v, ln_v_g[l], ln_v_b[l], w_spatial[l], b_spatial[l]) @ w_proj_b[l]
        merged = jax.nn.sigmoid(ga) * y_a + jax.nn.sigmoid(gb) * y_b
        x = x + merged @ w_out[l]
        h2 = rms_norm(x, norm2_g[l]).reshape(B * T, D)
        y = hier_moe(h2, router_group_w[l], router_group_b[l], router_expert_w[l], router_expert_b[l], w_gate[l], w_up[l], w_down[l])
        x = x + y.reshape(B, T, D)
    return rms_norm(x, final_norm_g)
```

```python
import functools

import numpy as np
import jax
import jax.numpy as jnp
from jax import lax
from jax.experimental import pallas as pl
from jax.experimental.pallas import tpu as pltpu

F32 = jnp.float32
BF16 = jnp.bfloat16

N_HEADS = 8
HEAD_DIM = 64
KV_RANK = 128
IDX_HEADS = 8
IDX_DIM = 64
TOPK_MAX = 256
CHUNK = 128
GMLP_GROUPS = 4
GMLP_GROUP_DIM = 128
GMLP_WIDTH = GMLP_GROUPS * GMLP_GROUP_DIM
NUM_BUCKETS = 32
MAX_DISTANCE = 128
N_GROUPS = 4
EXPERTS_PER_GROUP = 8
N_EXPERTS = N_GROUPS * EXPERTS_PER_GROUP
D_FF_EXPERT = 256
EPS = 1e-6

LANES = 128
VMEM_LIMIT = 52 * 1024 * 1024
NEG = -3.0e38

TM_PROJ = 256
Q_BLK = 256
TM_MIX = 256
TM_MOE = 1024
BISECT_ITERS = 40


def _dot(a, b):
    return jnp.dot(a, b, preferred_element_type=F32)


def _dot_t(a, b):
    return lax.dot_general(a, b, (((1,), (1,)), ((), ())), preferred_element_type=F32)


def _rms(x, g):
    return x * lax.rsqrt(jnp.mean(x * x, axis=-1, keepdims=True) + EPS) * g


def _proj_kernel(x_ref, g1_ref, wq_ref, wuk_ref, wc_ref, gkv_ref, wqi_ref, wkw_ref, wu_ref,
                 wv_ref, lng_ref, lnb_ref, wga_ref, wgb_ref,
                 qlat_ref, ckv_ref, qi_ref, ki_ref, wi_ref, u_ref, v_ref, ga_ref, gb_ref):
    hb = _rms(x_ref[...], g1_ref[...]).astype(BF16)
    q = _dot(hb, wq_ref[...])
    qlat_ref[...] = (_dot(q.astype(BF16), wuk_ref[...]) * (HEAD_DIM ** -0.5)).astype(BF16)
    ckv_ref[...] = _rms(_dot(hb, wc_ref[...]), gkv_ref[...]).astype(BF16)
    qi = _dot(hb, wqi_ref[...])
    for h in range(IDX_HEADS):
        qi_ref[h] = qi[:, h * IDX_DIM:(h + 1) * IDX_DIM].astype(BF16)
    kw = _dot(hb, wkw_ref[...])
    ki_ref[...] = kw[:, :IDX_DIM].astype(BF16)
    wi_ref[...] = kw * ((IDX_HEADS ** -0.5) * (IDX_DIM ** -0.5))
    u_ref[...] = jax.nn.gelu(_dot(hb, wu_ref[...])).astype(BF16)
    v = jax.nn.gelu(_dot(hb, wv_ref[...]))
    mu = jnp.mean(v, axis=-1, keepdims=True)
    var = jnp.mean(jnp.square(v - mu), axis=-1, keepdims=True)
    v_ref[...] = ((v - mu) * lax.rsqrt(var + EPS) * lng_ref[...] + lnb_ref[...]).astype(BF16)
    ga_ref[...] = jax.nn.sigmoid(_dot(hb, wga_ref[...])).astype(BF16)
    gb_ref[...] = jax.nn.sigmoid(_dot(hb, wgb_ref[...])).astype(BF16)


def _const_spec(shape):
    nd = len(shape)
    return pl.BlockSpec(shape, lambda *_: (0,) * nd)


def _projection(x2, g1, wq, wuk, wc, gkv, wqi, wkw, wu, wv, lng, lnb, wga, wgb):
    n, d = x2.shape
    tm = TM_PROJ
    row = lambda w: pl.BlockSpec((tm, w), lambda i: (i, 0))
    weights = (g1, wq, wuk, wc, gkv, wqi, wkw, wu, wv, lng, lnb, wga, wgb)
    out_shape = (
        jax.ShapeDtypeStruct((n, N_HEADS * KV_RANK), BF16),
        jax.ShapeDtypeStruct((n, KV_RANK), BF16),
        jax.ShapeDtypeStruct((IDX_HEADS, n, IDX_DIM), BF16),
        jax.ShapeDtypeStruct((n, IDX_DIM), BF16),
        jax.ShapeDtypeStruct((n, LANES), F32),
        jax.ShapeDtypeStruct((n, GMLP_WIDTH), BF16),
        jax.ShapeDtypeStruct((n, GMLP_WIDTH), BF16),
        jax.ShapeDtypeStruct((n, d), BF16),
        jax.ShapeDtypeStruct((n, d), BF16),
    )
    out_specs = (
        row(N_HEADS * KV_RANK), row(KV_RANK),
        pl.BlockSpec((IDX_HEADS, tm, IDX_DIM), lambda i: (0, i, 0)),
        row(IDX_DIM), row(LANES), row(GMLP_WIDTH), row(GMLP_WIDTH), row(d), row(d),
    )
    return pl.pallas_call(
        _proj_kernel,
        grid=(n // tm,),
        in_specs=[row(d)] + [_const_spec(w.shape) for w in weights],
        out_specs=out_specs,
        out_shape=out_shape,
        compiler_params=pltpu.CompilerParams(
            dimension_semantics=("arbitrary",), vmem_limit_bytes=VMEM_LIMIT),
        name="in_projection",
    )(x2, *weights)


def _attn_kernel(qlat_ref, qi_ref, wi_ref, ckv_ref, ki_ref, d0_ref, d1_ref, o_ref,
                 idx_ref, wbig_ref, m_ref, l_ref, acc_ref, *, topk, seq):
    qb = Q_BLK
    nh = N_HEADS
    j_q = pl.program_id(1)
    q0 = j_q * qb
    nch = j_q + 1
    kf = float(topk)

    wi = wi_ref[...]
    for h in range(IDX_HEADS):
        wbig_ref[h * qb:(h + 1) * qb, :] = jnp.broadcast_to(
            wi[:, IDX_DIM + h:IDX_DIM + h + 1], (qb, qb))

    qi = qi_ref[...].reshape(IDX_HEADS * qb, IDX_DIM)
    row = lax.broadcasted_iota(jnp.int32, (qb, qb), 0)
    col = lax.broadcasted_iota(jnp.int32, (qb, qb), 1)

    def idx_chunk(j, carry):
        rmax, rmin = carry
        k0 = pl.multiple_of(j * qb, qb)
        z = _dot_t(qi, ki_ref[pl.ds(k0, qb), :])
        zw = jnp.maximum(z, 0.0) * wbig_ref[...]
        index = zw[0:qb]
        for h in range(1, IDX_HEADS):
            index = index + zw[h * qb:(h + 1) * qb]
        valid = (k0 + col) <= (q0 + row)
        idx_ref[:, pl.ds(k0, qb)] = jnp.where(valid, index, NEG)
        rmax = jnp.maximum(rmax, jnp.max(jnp.where(valid, index, NEG), axis=1, keepdims=True))
        rmin = jnp.minimum(rmin, jnp.min(jnp.where(valid, index, -NEG), axis=1, keepdims=True))
        return rmax, rmin

    rmax, rmin = lax.fori_loop(
        0, nch, idx_chunk, (jnp.full((qb, 1), NEG, F32), jnp.full((qb, 1), -NEG, F32)))

    def count_ge(thr):
        thr_b = jnp.broadcast_to(thr, (qb, LANES))

        def body(j, c):
            k0 = pl.multiple_of(j * qb, qb)
            for s in range(qb // LANES):
                xc = idx_ref[:, pl.ds(k0 + s * LANES, LANES)]
                c = c + jnp.where(xc >= thr_b, 1.0, 0.0)
            return c

        c = lax.fori_loop(0, nch, body, jnp.zeros((qb, LANES), F32))
        return jnp.sum(c, axis=1, keepdims=True)

    nvalid = (q0 + 1 + lax.broadcasted_iota(jnp.int32, (qb, 1), 0)).astype(F32)
    hi0 = rmax + (jnp.abs(rmax) * 1e-6 + 1e-30)

    def bisect(_, carry):
        lo, hi, c_lo, c_hi = carry
        mid = 0.5 * (lo + hi)
        c = count_ge(mid)
        ok = c >= kf
        return (jnp.where(ok, mid, lo), jnp.where(ok, hi, mid),
                jnp.where(ok, c, c_lo), jnp.where(ok, c_hi, c))

    lo, hi, c_lo, c_hi = lax.fori_loop(
        0, BISECT_ITERS, bisect, (rmin, hi0, nvalid, jnp.zeros((qb, 1), F32)))

    excess = c_lo > kf

    @pl.when(jnp.max(jnp.where(excess, 1.0, 0.0)) > 0.0)
    def _():
        need = kf - c_hi
        lo_b = jnp.broadcast_to(lo, (qb, LANES))
        hi_b = jnp.broadcast_to(hi, (qb, LANES))
        lane = lax.broadcasted_iota(jnp.int32, (qb, LANES), 1)

        def count_le(pos):
            pos_b = jnp.broadcast_to(pos, (qb, LANES))

            def body(j, c):
                k0 = pl.multiple_of(j * qb, qb)
                for s in range(qb // LANES):
                    xc = idx_ref[:, pl.ds(k0 + s * LANES, LANES)]
                    inb = (xc >= lo_b) & (xc < hi_b) & ((k0 + s * LANES + lane) <= pos_b)
                    c = c + jnp.where(inb, 1.0, 0.0)
                return c

            c = lax.fori_loop(0, nch, body, jnp.zeros((qb, LANES), F32))
            return jnp.sum(c, axis=1, keepdims=True)

        def pbisect(_, carry):
            plo, phi = carry
            pm = (plo + phi) >> 1
            ok = count_le(pm) >= need
            return jnp.where(ok, plo, pm), jnp.where(ok, pm, phi)

        nbits = int(np.ceil(np.log2(seq + 1)))
        _, pcut = lax.fori_loop(
            0, nbits, pbisect,
            (jnp.full((qb, 1), -1, jnp.int32), jnp.full((qb, 1), seq - 1, jnp.int32)))
        pcut_b = jnp.broadcast_to(jnp.where(excess, pcut, seq), (qb, LANES))

        def demote(j, carry):
            k0 = pl.multiple_of(j * qb, qb)
            for s in range(qb // LANES):
                xc = idx_ref[:, pl.ds(k0 + s * LANES, LANES)]
                drop = (xc >= lo_b) & (xc < hi_b) & ((k0 + s * LANES + lane) > pcut_b)
                idx_ref[:, pl.ds(k0 + s * LANES, LANES)] = jnp.where(drop, NEG, xc)
            return carry

        lax.fori_loop(0, nch, demote, 0)

    ql = qlat_ref[...]
    q_all = jnp.concatenate([ql[:, h * KV_RANK:(h + 1) * KV_RANK] for h in range(nh)], axis=0)
    m_ref[...] = jnp.full(m_ref.shape, NEG, F32)
    l_ref[...] = jnp.zeros(l_ref.shape, F32)
    acc_ref[...] = jnp.zeros(acc_ref.shape, F32)
    lo_q = jnp.broadcast_to(lo, (qb, qb))

    def att_step(j, bias_ref):
        k0 = pl.multiple_of(j * qb, qb)
        ckc = ckv_ref[pl.ds(k0, qb), :]
        s = _dot_t(q_all, ckc)
        if bias_ref is not None:
            s = s + bias_ref[...]
        sel = idx_ref[:, pl.ds(k0, qb)] >= lo_q
        s = jnp.where(sel[None], s.reshape(nh, qb, qb), NEG).reshape(nh * qb, qb)
        m_old = m_ref[...]
        m_new = jnp.maximum(m_old, jnp.max(s, axis=1, keepdims=True))
        alpha = jnp.exp(m_old - m_new)
        p = jnp.exp(s - m_new)
        l_ref[...] = alpha * l_ref[...] + jnp.sum(p, axis=1, keepdims=True)
        acc_ref[...] = alpha * acc_ref[...] + _dot(p.astype(BF16), ckc)
        m_ref[...] = m_new

    def far_step(j, carry):
        att_step(j, None)
        return carry

    lax.fori_loop(0, jnp.maximum(nch - 2, 0), far_step, 0)

    @pl.when(nch >= 2)
    def _():
        att_step(nch - 2, d1_ref)

    att_step(nch - 1, d0_ref)

    o = acc_ref[...] / l_ref[...]
    for h in range(nh):
        o_ref[:, h * KV_RANK:(h + 1) * KV_RANK] = o[h * qb:(h + 1) * qb].astype(BF16)


def _attention(qlat, qi, wi, ckv, ki, d0, d1, batch, seq):
    qb = Q_BLK
    nq = seq // qb
    n = batch * seq
    topk = min(TOPK_MAX, seq // 4)
    kern = functools.partial(_attn_kernel, topk=topk, seq=seq)
    return pl.pallas_call(
        kern,
        grid=(batch, nq),
        in_specs=[
            pl.BlockSpec((qb, N_HEADS * KV_RANK), lambda b, j: (b * nq + j, 0)),
            pl.BlockSpec((IDX_HEADS, qb, IDX_DIM), lambda b, j: (0, b * nq + j, 0)),
            pl.BlockSpec((qb, LANES), lambda b, j: (b * nq + j, 0)),
            pl.BlockSpec((seq, KV_RANK), lambda b, j: (b, 0)),
            pl.BlockSpec((seq, IDX_DIM), lambda b, j: (b, 0)),
            _const_spec(d0.shape), _const_spec(d1.shape),
        ],
        out_specs=pl.BlockSpec((qb, N_HEADS * KV_RANK), lambda b, j: (b * nq + j, 0)),
        out_shape=jax.ShapeDtypeStruct((n, N_HEADS * KV_RANK), BF16),
        scratch_shapes=[
            pltpu.VMEM((qb, seq), F32),
            pltpu.VMEM((IDX_HEADS * qb, qb), F32),
            pltpu.VMEM((N_HEADS * qb, 1), F32),
            pltpu.VMEM((N_HEADS * qb, 1), F32),
            pltpu.VMEM((N_HEADS * qb, KV_RANK), F32),
        ],
        compiler_params=pltpu.CompilerParams(
            dimension_semantics=("arbitrary", "arbitrary"), vmem_limit_bytes=VMEM_LIMIT),
        name="sparse_attention",
    )(qlat, qi, wi, ckv, ki, d0, d1)


def _mix_kernel(x_ref, ol_ref, u_ref, v_ref, ga_ref, gb_ref, wuv_ref, wpa_ref, ws_ref, bs_ref,
                wpb_ref, wout_ref, g2_ref, wr_ref, br_ref, x1_ref, h2_ref, comb_ref):
    tm = x_ref.shape[0]
    o_a = _dot(ol_ref[...], wuv_ref[...]).astype(BF16)
    y_a = _dot(o_a, wpa_ref[...])
    v = v_ref[...]
    u = u_ref[...].astype(F32)
    rows = []
    for c in range(tm // CHUNK):
        cols = []
        for g in range(GMLP_GROUPS):
            vc = v[c * CHUNK:(c + 1) * CHUNK, g * GMLP_GROUP_DIM:(g + 1) * GMLP_GROUP_DIM]
            cols.append(_dot(ws_ref[g], vc) + bs_ref[g])
        rows.append(jnp.concatenate(cols, axis=1))
    s = jnp.concatenate(rows, axis=0)
    y_b = _dot((u * s).astype(BF16), wpb_ref[...])
    merged = ga_ref[...].astype(F32) * y_a + gb_ref[...].astype(F32) * y_b
    x1 = x_ref[...] + _dot(merged.astype(BF16), wout_ref[...])
    x1_ref[...] = x1
    h2 = _rms(x1, g2_ref[...])
    h2_ref[...] = h2.astype(BF16)

    logits = jnp.dot(h2, wr_ref[...], preferred_element_type=F32,
                     precision=lax.Precision.HIGHEST) + br_ref[...]
    lane = lax.broadcasted_iota(jnp.int32, logits.shape, 1)
    big = jnp.int32(4 * LANES)
    is_g = (lane >= N_EXPERTS) & (lane < N_EXPERTS + N_GROUPS)
    lg = jnp.where(is_g, logits, NEG)
    gmax = jnp.max(lg, axis=1, keepdims=True)
    g_w = 1.0 / jnp.sum(jnp.where(is_g, jnp.exp(lg - gmax), 0.0), axis=1, keepdims=True)
    g_idx = jnp.min(jnp.where(is_g & (lg == gmax), lane, big), axis=1, keepdims=True) - N_EXPERTS
    in_g = (lane < N_EXPERTS) & ((lane >> 3) == g_idx)
    le = jnp.where(in_g, logits, NEG)
    emax = jnp.max(le, axis=1, keepdims=True)
    ee = jnp.where(in_g, jnp.exp(le - emax), 0.0)
    ep = ee / jnp.sum(ee, axis=1, keepdims=True)
    ep1 = jnp.where(in_g, ep, -1.0)
    m1 = jnp.max(ep1, axis=1, keepdims=True)
    i1 = jnp.min(jnp.where(ep1 == m1, lane, big), axis=1, keepdims=True)
    ep2 = jnp.where(lane == i1, -1.0, ep1)
    m2 = jnp.max(ep2, axis=1, keepdims=True)
    i2 = jnp.min(jnp.where(ep2 == m2, lane, big), axis=1, keepdims=True)
    tot = m1 + m2
    comb_ref[...] = (jnp.where(lane == i1, g_w * m1 / tot, 0.0)
                     + jnp.where(lane == i2, g_w * m2 / tot, 0.0))


def _mix(x2, olat, u, v, ga, gb, wuv, wpa, ws, bs, wpb, wout, g2, wr, br):
    n, d = x2.shape
    tm = TM_MIX
    row = lambda w: pl.BlockSpec((tm, w), lambda i: (i, 0))
    weights = (wuv, wpa, ws, bs, wpb, wout, g2, wr, br)
    return pl.pallas_call(
        _mix_kernel,
        grid=(n // tm,),
        in_specs=[row(d), row(N_HEADS * KV_RANK), row(GMLP_WIDTH), row(GMLP_WIDTH), row(d), row(d)]
        + [_const_spec(w.shape) for w in weights],
        out_specs=(row(d), row(d), row(LANES)),
        out_shape=(jax.ShapeDtypeStruct((n, d), F32), jax.ShapeDtypeStruct((n, d), BF16),
                   jax.ShapeDtypeStruct((n, LANES), F32)),
        compiler_params=pltpu.CompilerParams(
            dimension_semantics=("arbitrary",), vmem_limit_bytes=VMEM_LIMIT),
        name="merge_router",
    )(x2, olat, u, v, ga, gb, *weights)


def _moe_kernel(x1_ref, h2_ref, comb_ref, wg_ref, wu_ref, wd_ref, gf_ref, o_ref, acc_ref):
    e = pl.program_id(1)

    @pl.when(e == 0)
    def _():
        acc_ref[...] = jnp.zeros(acc_ref.shape, F32)

    h = h2_ref[...]
    comb = comb_ref[...]
    lane = lax.broadcasted_iota(jnp.int32, comb.shape, 1)
    w_e = jnp.sum(jnp.where(lane == e, comb, 0.0), axis=1, keepdims=True)
    act = jax.nn.silu(_dot(h, wg_ref[0])) * _dot(h, wu_ref[0]) * w_e
    acc_ref[...] += _dot(act.astype(BF16), wd_ref[0])

    @pl.when(e == pl.num_programs(1) - 1)
    def _():
        o_ref[...] = _rms(x1_ref[...] + acc_ref[...], gf_ref[...])


def _moe(x1, h2, comb, wg, wu, wd, gf):
    n, d = x1.shape
    tm = min(TM_MOE, n)
    row = lambda w: pl.BlockSpec((tm, w), lambda i, e: (i, 0))
    return pl.pallas_call(
        _moe_kernel,
        grid=(n // tm, N_EXPERTS),
        in_specs=[row(d), row(d), row(LANES),
                  pl.BlockSpec((1, d, D_FF_EXPERT), lambda i, e: (e, 0, 0)),
                  pl.BlockSpec((1, d, D_FF_EXPERT), lambda i, e: (e, 0, 0)),
                  pl.BlockSpec((1, D_FF_EXPERT, d), lambda i, e: (e, 0, 0)),
                  pl.BlockSpec((1, d), lambda i, e: (0, 0))],
        out_specs=row(d),
        out_shape=jax.ShapeDtypeStruct((n, d), F32),
        scratch_shapes=[pltpu.VMEM((tm, d), F32)],
        compiler_params=pltpu.CompilerParams(
            dimension_semantics=("arbitrary", "arbitrary"), vmem_limit_bytes=VMEM_LIMIT),
        name="experts",
    )(x1, h2, comb, wg, wu, wd, gf)


def _t5_bucket(n):
    max_exact = NUM_BUCKETS // 2
    nf = jnp.maximum(n, 1).astype(F32)
    large = max_exact + (jnp.log(nf / max_exact) / np.float32(np.log(MAX_DISTANCE / max_exact))
                         * (NUM_BUCKETS - max_exact)).astype(jnp.int32)
    large = jnp.minimum(large, NUM_BUCKETS - 1)
    return jnp.where(n < max_exact, n, large)


def _bias_tiles(rel_bias):
    qb = Q_BLK
    r = jnp.arange(qb, dtype=jnp.int32)[:, None]
    c = jnp.arange(qb, dtype=jnp.int32)[None, :]
    far = rel_bias[NUM_BUCKETS - 1]
    d0 = rel_bias[_t5_bucket(jnp.maximum(r - c, 0))] - far
    d1 = rel_bias[_t5_bucket(qb + r - c)] - far
    to_rows = lambda t: jnp.transpose(t, (2, 0, 1)).reshape(N_HEADS * qb, qb)
    return to_rows(d0), to_rows(d1)


def _block_diag(w):
    h, a, b = w.shape
    eye = jnp.eye(h, dtype=w.dtype)
    return (eye[:, None, :, None] * w[:, :, None, :]).reshape(h * a, h * b)


def kernel(x, w_in, kv_norm_g, w_uk, w_uv, rel_bias, ln_v_g, ln_v_b, w_spatial, b_spatial,
           w_proj_a, w_proj_b, w_out, norm1_g, norm2_g, router_group_w, router_group_b,
           router_expert_w, router_expert_b, w_gate, w_up, w_down, final_norm_g):
    batch, seq, d = x.shape
    depth = w_in.shape[0]
    n = batch * seq
    assert depth == 1, "the final rms-norm is fused into the (single) layer's expert kernel"
    assert EXPERTS_PER_GROUP == 8
    assert seq % Q_BLK == 0 and n % TM_PROJ == 0 and n % TM_MIX == 0 and n % min(TM_MOE, n) == 0
    assert Q_BLK + 1 >= MAX_DISTANCE

    attn_w = N_HEADS * HEAD_DIM
    sizes = (attn_w, KV_RANK, IDX_HEADS * IDX_DIM, IDX_DIM, IDX_HEADS, GMLP_WIDTH, GMLP_WIDTH, d, d)
    offs = np.concatenate([[0], np.cumsum(sizes)])
    d0, d1 = _bias_tiles(rel_bias)
    tril = jnp.tril(jnp.ones((CHUNK, CHUNK), dtype=bool))

    x2 = x.reshape(n, d)
    for l in range(depth):
        wl = w_in[l].astype(BF16)
        wq, wc, wqi, wki, wwi, wu, wv, wga, wgb = [wl[:, offs[i]:offs[i + 1]] for i in range(9)]
        wkw = jnp.concatenate(
            [wki, wwi, jnp.zeros((d, LANES - IDX_DIM - IDX_HEADS), BF16)], axis=1)
        wuk_bd = _block_diag(jnp.transpose(w_uk[l], (1, 2, 0))).astype(BF16)
        wuv_bd = _block_diag(jnp.transpose(w_uv[l], (1, 0, 2))).astype(BF16)
        row = lambda a: a.reshape(1, -1)

        qlat, ckv, qi, ki, wi, u, v, ga, gb = _projection(
            x2, row(norm1_g[l]), wq, wuk_bd, wc, row(kv_norm_g[l]), wqi, wkw, wu, wv,
            row(ln_v_g[l]), row(ln_v_b[l]), wga, wgb)

        olat = _attention(qlat, qi, wi, ckv, ki, d0, d1, batch, seq)

        ws = jnp.where(tril[None], w_spatial[l], 0.0).astype(BF16)
        bs = jnp.broadcast_to(b_spatial[l][:, :, None], (GMLP_GROUPS, CHUNK, GMLP_GROUP_DIM))
        wr = jnp.concatenate(
            [router_expert_w[l], router_group_w[l],
             jnp.zeros((d, LANES - N_EXPERTS - N_GROUPS), F32)], axis=1)
        br = jnp.concatenate(
            [router_expert_b[l], router_group_b[l],
             jnp.zeros((LANES - N_EXPERTS - N_GROUPS,), F32)]).reshape(1, LANES)
        x1, h2, comb = _mix(
            x2, olat, u, v, ga, gb, wuv_bd, w_proj_a[l].astype(BF16), ws, bs,
            w_proj_b[l].astype(BF16), w_out[l].astype(BF16), row(norm2_g[l]), wr, br)

        x2 = _moe(x1, h2, comb, w_gate[l].astype(BF16), w_up[l].astype(BF16),
                  w_down[l].astype(BF16), row(final_norm_g))
    return x2.reshape(batch, seq, d)
```

```python
import functools

import numpy as np
import jax
import jax.numpy as jnp
from jax import lax
from jax.experimental import pallas as pl
from jax.experimental.pallas import tpu as pltpu

F32 = jnp.float32
BF16 = jnp.bfloat16

N_HEADS = 8
HEAD_DIM = 64
KV_RANK = 128
IDX_HEADS = 8
IDX_DIM = 64
TOPK_MAX = 256
CHUNK = 128
GMLP_GROUPS = 4
GMLP_GROUP_DIM = 128
GMLP_WIDTH = GMLP_GROUPS * GMLP_GROUP_DIM
NUM_BUCKETS = 32
MAX_DISTANCE = 128
N_GROUPS = 4
EXPERTS_PER_GROUP = 8
N_EXPERTS = N_GROUPS * EXPERTS_PER_GROUP
D_FF_EXPERT = 256
EPS = 1e-6

LANES = 128
SUBLANES = 8
VMEM_LIMIT = 52 * 1024 * 1024
NEG = -3.0e38
MASK = -1.0e30

TM_PROJ = 256
Q_BLK = 256
TM_MIX = 256
TM_MOE = 1024
SEARCH_MAX_ITERS = 64
VERIFY_FROM = 23
VERIFY_EVERY = 8


def _dot(a, b):
    return jnp.dot(a, b, preferred_element_type=F32)


def _dot_t(a, b):
    return lax.dot_general(a, b, (((1,), (1,)), ((), ())), preferred_element_type=F32)


def _rms(x, g):
    return x * lax.rsqrt(jnp.mean(x * x, axis=-1, keepdims=True) + EPS) * g


_T_ROWS = (N_HEADS * HEAD_DIM, IDX_HEADS * IDX_DIM, KV_RANK, IDX_HEADS)
_T_OFFS = tuple(int(v) for v in np.concatenate([[0], np.cumsum(_T_ROWS)]))


def _proj_kernel(x_ref, g1_ref, wt_ref, wuk_ref, gkvc_ref, wc_ref, gkv_ref, wki_ref, wu_ref,
                 wv_ref, lng_ref, lnb_ref, wga_ref, wgb_ref,
                 qlat_ref, qi_ref, wi_ref, ckvt_ref, ckv_ref, ki_ref, u_ref, v_ref, ga_ref, gb_ref):
    hb = _rms(x_ref[...], g1_ref[...]).astype(BF16)
    t_all = _dot_t(wt_ref[...], hb)
    q_t, qi_t, c_t, w_t = [t_all[_T_OFFS[i]:_T_OFFS[i + 1]] for i in range(4)]
    qlat_t = (_dot(wuk_ref[...], q_t.astype(BF16)) * (HEAD_DIM ** -0.5)).astype(BF16)
    w_t = w_t * ((IDX_HEADS ** -0.5) * (IDX_DIM ** -0.5))
    tm = x_ref.shape[0]
    for h in range(N_HEADS):
        cols = slice(h * tm, (h + 1) * tm)
        qlat_ref[:, cols] = qlat_t[h * KV_RANK:(h + 1) * KV_RANK]
        qi_ref[:, cols] = qi_t[h * IDX_DIM:(h + 1) * IDX_DIM].astype(BF16)
        wi_ref[:, cols] = w_t[h:h + 1]
    c_n = c_t * lax.rsqrt(jnp.mean(c_t * c_t, axis=0, keepdims=True) + EPS) * gkvc_ref[...]
    ckvt_ref[...] = c_n.astype(BF16)
    ckv_ref[...] = _rms(_dot(hb, wc_ref[...]), gkv_ref[...]).astype(BF16)
    ki_ref[...] = _dot(hb, wki_ref[...]).astype(BF16)
    u_ref[...] = jax.nn.gelu(_dot(hb, wu_ref[...])).astype(BF16)
    v = jax.nn.gelu(_dot(hb, wv_ref[...]))
    mu = jnp.mean(v, axis=-1, keepdims=True)
    var = jnp.mean(jnp.square(v - mu), axis=-1, keepdims=True)
    v_ref[...] = ((v - mu) * lax.rsqrt(var + EPS) * lng_ref[...] + lnb_ref[...]).astype(BF16)
    ga_ref[...] = jax.nn.sigmoid(_dot(hb, wga_ref[...])).astype(BF16)
    gb_ref[...] = jax.nn.sigmoid(_dot(hb, wgb_ref[...])).astype(BF16)


def _const_spec(shape):
    nd = len(shape)
    return pl.BlockSpec(shape, lambda *_: (0,) * nd)


def _projection(x2, g1, wt, wuk, gkvc, wc, gkv, wki, wu, wv, lng, lnb, wga, wgb):
    n, d = x2.shape
    tm = TM_PROJ
    row = lambda w: pl.BlockSpec((tm, w), lambda i: (i, 0))
    col = lambda h: pl.BlockSpec((h, tm), lambda i: (0, i))
    wide = lambda h: pl.BlockSpec((h, N_HEADS * tm), lambda i: (0, i))
    weights = (g1, wt, wuk, gkvc, wc, gkv, wki, wu, wv, lng, lnb, wga, wgb)
    out_shape = (
        jax.ShapeDtypeStruct((KV_RANK, N_HEADS * n), BF16),
        jax.ShapeDtypeStruct((IDX_DIM, IDX_HEADS * n), BF16),
        jax.ShapeDtypeStruct((1, IDX_HEADS * n), F32),
        jax.ShapeDtypeStruct((KV_RANK, n), BF16),
        jax.ShapeDtypeStruct((n, KV_RANK), BF16),
        jax.ShapeDtypeStruct((n, IDX_DIM), BF16),
        jax.ShapeDtypeStruct((n, GMLP_WIDTH), BF16),
        jax.ShapeDtypeStruct((n, GMLP_WIDTH), BF16),
        jax.ShapeDtypeStruct((n, d), BF16),
        jax.ShapeDtypeStruct((n, d), BF16),
    )
    out_specs = (
        wide(KV_RANK), wide(IDX_DIM), wide(1), col(KV_RANK),
        row(KV_RANK), row(IDX_DIM), row(GMLP_WIDTH), row(GMLP_WIDTH), row(d), row(d),
    )
    return pl.pallas_call(
        _proj_kernel,
        grid=(n // tm,),
        in_specs=[row(d)] + [_const_spec(w.shape) for w in weights],
        out_specs=out_specs,
        out_shape=out_shape,
        compiler_params=pltpu.CompilerParams(
            dimension_semantics=("arbitrary",), vmem_limit_bytes=VMEM_LIMIT),
        name="in_projection",
    )(x2, *weights)


def _attn_kernel(qlat_ref, qi_ref, wi_ref, ckv_ref, ckvt_ref, ki_ref, d0_ref, d1_ref, o_ref,
                 idx_ref, m_ref, l_ref, acc_ref, *, topk, seq):
    qb = Q_BLK
    ns = qb // SUBLANES
    j_q = pl.program_id(1)
    q0 = j_q * qb
    nch = j_q + 1
    kf = float(topk)

    def chunk_rows(j):
        return pl.ds(pl.multiple_of(j * qb, qb), qb)

    krow = lax.broadcasted_iota(jnp.int32, (qb, qb), 0)
    qcol = lax.broadcasted_iota(jnp.int32, (qb, qb), 1)

    def idx_chunk(j, carry):
        rmax, rmin = carry
        zw = jnp.maximum(_dot(ki_ref[chunk_rows(j), :], qi_ref[...]), 0.0) * wi_ref[...]
        index = zw[:, 0:qb]
        for h in range(1, IDX_HEADS):
            index = index + zw[:, h * qb:(h + 1) * qb]
        valid = (j * qb + krow) <= (q0 + qcol)
        x = jnp.where(valid, index, NEG)
        idx_ref[chunk_rows(j), :] = x
        rmax = jnp.maximum(rmax, jnp.max(x, axis=0, keepdims=True))
        rmin = jnp.minimum(rmin, jnp.min(jnp.where(valid, index, -NEG), axis=0, keepdims=True))
        return rmax, rmin

    rmax, rmin = lax.fori_loop(
        0, nch, idx_chunk, (jnp.full((1, qb), NEG, F32), jnp.full((1, qb), -NEG, F32)))

    def slabs(j):
        return idx_ref[chunk_rows(j), :].reshape(ns, SUBLANES, qb)

    def col_sum(c8):
        return jnp.sum(c8, axis=0, keepdims=True)

    def count_ge(thr):
        thr8 = jnp.broadcast_to(thr, (SUBLANES, qb))[None]

        def body(j, c8):
            return c8 + jnp.sum(jnp.where(slabs(j) >= thr8, 1.0, 0.0), axis=0)

        return col_sum(lax.fori_loop(0, nch, body, jnp.zeros((SUBLANES, qb), F32)))

    def bracket_is_tie(lo, hi):
        lo8 = jnp.broadcast_to(lo, (SUBLANES, qb))[None]
        hi8 = jnp.broadcast_to(hi, (SUBLANES, qb))[None]

        def body(j, carry):
            bmax, bmin = carry
            xs = slabs(j)
            inb = (xs >= lo8) & (xs < hi8)
            bmax = jnp.maximum(bmax, jnp.max(jnp.where(inb, xs, NEG), axis=0))
            bmin = jnp.minimum(bmin, jnp.min(jnp.where(inb, xs, -NEG), axis=0))
            return bmax, bmin

        bmax, bmin = lax.fori_loop(
            0, nch, body,
            (jnp.full((SUBLANES, qb), NEG, F32), jnp.full((SUBLANES, qb), -NEG, F32)))
        bmax = jnp.max(bmax, axis=0, keepdims=True)
        bmin = jnp.min(bmin, axis=0, keepdims=True)
        return jnp.where(bmax == bmin, 1.0, 0.0)

    nvalid = (q0 + 1 + lax.broadcasted_iota(jnp.int32, (1, qb), 1)).astype(F32)
    hi0 = rmax + (jnp.abs(rmax) * 1e-6 + 1e-30)

    def open_rows(c_lo, tie):
        return jnp.max(jnp.where((c_lo > kf) & (tie < 0.5), 1.0, 0.0)) > 0.0

    def search_cond(state):
        it, _, _, c_lo, _, tie = state
        return (it < SEARCH_MAX_ITERS) & open_rows(c_lo, tie)

    def search_body(state):
        it, lo, hi, c_lo, c_hi, tie = state
        mid = 0.5 * (lo + hi)
        c = count_ge(mid)
        ok = c >= kf
        lo, hi = jnp.where(ok, mid, lo), jnp.where(ok, hi, mid)
        c_lo, c_hi = jnp.where(ok, c, c_lo), jnp.where(ok, c_hi, c)
        check = (it >= VERIFY_FROM) & (((it - VERIFY_FROM) & (VERIFY_EVERY - 1)) == 0)
        tie = lax.cond(check, lambda: bracket_is_tie(lo, hi), lambda: tie)
        return it + 1, lo, hi, c_lo, c_hi, tie

    _, lo, hi, c_lo, c_hi, _ = lax.while_loop(
        search_cond, search_body,
        (jnp.int32(0), rmin, hi0, nvalid, jnp.zeros((1, qb), F32), jnp.zeros((1, qb), F32)))

    excess = c_lo > kf

    @pl.when(jnp.max(jnp.where(excess, 1.0, 0.0)) > 0.0)
    def _():
        need = kf - c_hi
        lo8 = jnp.broadcast_to(lo, (SUBLANES, qb))[None]
        hi8 = jnp.broadcast_to(hi, (SUBLANES, qb))[None]
        kpos = (lax.broadcasted_iota(jnp.int32, (ns, SUBLANES, qb), 0) * SUBLANES
                + lax.broadcasted_iota(jnp.int32, (ns, SUBLANES, qb), 1))

        def count_le(pos):
            pos8 = jnp.broadcast_to(pos, (SUBLANES, qb))[None]

            def body(j, c8):
                xs = slabs(j)
                inb = (xs >= lo8) & (xs < hi8) & ((j * qb + kpos) <= pos8)
                return c8 + jnp.sum(jnp.where(inb, 1.0, 0.0), axis=0)

            return col_sum(lax.fori_loop(0, nch, body, jnp.zeros((SUBLANES, qb), F32)))

        def pbisect(_, carry):
            plo, phi = carry
            pm = (plo + phi) >> 1
            ok = count_le(pm) >= need
            return jnp.where(ok, plo, pm), jnp.where(ok, pm, phi)

        nbits = int(np.ceil(np.log2(seq + 1)))
        _, pcut = lax.fori_loop(
            0, nbits, pbisect,
            (jnp.full((1, qb), -1, jnp.int32), jnp.full((1, qb), seq - 1, jnp.int32)))
        pcut8 = jnp.broadcast_to(jnp.where(excess, pcut, seq), (SUBLANES, qb))[None]

        def demote(j, carry):
            xs = slabs(j)
            drop = (xs >= lo8) & (xs < hi8) & ((j * qb + kpos) > pcut8)
            idx_ref[chunk_rows(j), :] = jnp.where(drop, NEG, xs).reshape(qb, qb)
            return carry

        lax.fori_loop(0, nch, demote, 0)

    m_ref[...] = jnp.full(m_ref.shape, MASK, F32)
    l_ref[...] = jnp.zeros(l_ref.shape, F32)
    acc_ref[...] = jnp.zeros(acc_ref.shape, F32)

    def att_step(j, bias_ref):
        ckc = ckv_ref[chunk_rows(j), :]
        ckc_t = ckvt_ref[:, chunk_rows(j)]
        madd = jnp.where(idx_ref[chunk_rows(j), :] >= lo, 0.0, MASK)
        s_all = _dot(ckc, qlat_ref[...])
        s_h = []
        for h in range(N_HEADS):
            s = s_all[:, h * qb:(h + 1) * qb] + madd
            if bias_ref is not None:
                s = s + bias_ref[:, h * qb:(h + 1) * qb]
            s_h.append(s)
        s_all = jnp.concatenate(s_h, axis=1)
        m_old = m_ref[...]
        m_new = jnp.maximum(m_old, jnp.max(s_all, axis=0, keepdims=True))
        alpha = jnp.exp(m_old - m_new)
        p = jnp.exp(s_all - m_new)
        l_ref[...] = alpha * l_ref[...] + jnp.sum(p, axis=0, keepdims=True)
        acc_ref[...] = alpha * acc_ref[...] + _dot(ckc_t, p.astype(BF16))
        m_ref[...] = m_new

    def far_step(j, carry):
        att_step(j, None)
        return carry

    lax.fori_loop(0, jnp.maximum(nch - 2, 0), far_step, 0)

    @pl.when(nch >= 2)
    def _():
        att_step(nch - 2, d1_ref)

    att_step(nch - 1, d0_ref)

    o_ref[...] = (acc_ref[...] / l_ref[...]).astype(BF16)


def _attention(qlat_w, qi_w, wi_w, ckv, ckv_t, ki, d0, d1, batch, seq):
    qb = Q_BLK
    nq = seq // qb
    n = batch * seq
    topk = min(TOPK_MAX, seq // 4)
    kern = functools.partial(_attn_kernel, topk=topk, seq=seq)
    wide = lambda h: pl.BlockSpec((h, N_HEADS * qb), lambda b, j: (0, b * nq + j))
    return pl.pallas_call(
        kern,
        grid=(batch, nq),
        in_specs=[
            wide(KV_RANK), wide(IDX_DIM), wide(1),
            pl.BlockSpec((seq, KV_RANK), lambda b, j: (b, 0)),
            pl.BlockSpec((KV_RANK, seq), lambda b, j: (0, b)),
            pl.BlockSpec((seq, IDX_DIM), lambda b, j: (b, 0)),
            _const_spec(d0.shape), _const_spec(d1.shape),
        ],
        out_specs=wide(KV_RANK),
        out_shape=jax.ShapeDtypeStruct((KV_RANK, N_HEADS * n), BF16),
        scratch_shapes=[
            pltpu.VMEM((seq, qb), F32),
            pltpu.VMEM((1, N_HEADS * qb), F32),
            pltpu.VMEM((1, N_HEADS * qb), F32),
            pltpu.VMEM((KV_RANK, N_HEADS * qb), F32),
        ],
        compiler_params=pltpu.CompilerParams(
            dimension_semantics=("arbitrary", "arbitrary"), vmem_limit_bytes=VMEM_LIMIT),
        name="sparse_attention",
    )(qlat_w, qi_w, wi_w, ckv, ckv_t, ki, d0, d1)


def _mix_kernel(x_ref, ol_ref, u_ref, v_ref, ga_ref, gb_ref, wuv_ref, wpa_ref, ws_ref, bs_ref,
                wpb_ref, wout_ref, g2_ref, wr_ref, br_ref, x1_ref, h2_ref, comb_ref):
    tm = x_ref.shape[0]
    o_at = jnp.concatenate(
        [_dot(wuv_ref[h], ol_ref[:, h * tm:(h + 1) * tm]) for h in range(N_HEADS)], axis=0)
    y_a = _dot(jnp.transpose(o_at).astype(BF16), wpa_ref[...])
    v = v_ref[...]
    u = u_ref[...].astype(F32)
    rows = []
    for c in range(tm // CHUNK):
        cols = []
        for g in range(GMLP_GROUPS):
            vc = v[c * CHUNK:(c + 1) * CHUNK, g * GMLP_GROUP_DIM:(g + 1) * GMLP_GROUP_DIM]
            cols.append(_dot(ws_ref[g], vc) + bs_ref[g])
        rows.append(jnp.concatenate(cols, axis=1))
    s = jnp.concatenate(rows, axis=0)
    y_b = _dot((u * s).astype(BF16), wpb_ref[...])
    merged = ga_ref[...].astype(F32) * y_a + gb_ref[...].astype(F32) * y_b
    x1 = x_ref[...] + _dot(merged.astype(BF16), wout_ref[...])
    x1_ref[...] = x1
    h2 = _rms(x1, g2_ref[...])
    h2_ref[...] = h2.astype(BF16)

    logits = jnp.dot(h2, wr_ref[...], preferred_element_type=F32,
                     precision=lax.Precision.HIGHEST) + br_ref[...]
    lane = lax.broadcasted_iota(jnp.int32, logits.shape, 1)
    big = jnp.int32(4 * LANES)
    is_g = (lane >= N_EXPERTS) & (lane < N_EXPERTS + N_GROUPS)
    lg = jnp.where(is_g, logits, NEG)
    gmax = jnp.max(lg, axis=1, keepdims=True)
    g_w = 1.0 / jnp.sum(jnp.where(is_g, jnp.exp(lg - gmax), 0.0), axis=1, keepdims=True)
    g_idx = jnp.min(jnp.where(is_g & (lg == gmax), lane, big), axis=1, keepdims=True) - N_EXPERTS
    in_g = (lane < N_EXPERTS) & ((lane >> 3) == g_idx)
    le = jnp.where(in_g, logits, NEG)
    emax = jnp.max(le, axis=1, keepdims=True)
    ee = jnp.where(in_g, jnp.exp(le - emax), 0.0)
    ep = ee / jnp.sum(ee, axis=1, keepdims=True)
    ep1 = jnp.where(in_g, ep, -1.0)
    m1 = jnp.max(ep1, axis=1, keepdims=True)
    i1 = jnp.min(jnp.where(ep1 == m1, lane, big), axis=1, keepdims=True)
    ep2 = jnp.where(lane == i1, -1.0, ep1)
    m2 = jnp.max(ep2, axis=1, keepdims=True)
    i2 = jnp.min(jnp.where(ep2 == m2, lane, big), axis=1, keepdims=True)
    tot = m1 + m2
    comb_ref[...] = (jnp.where(lane == i1, g_w * m1 / tot, 0.0)
                     + jnp.where(lane == i2, g_w * m2 / tot, 0.0))


def _mix(x2, olat_t, u, v, ga, gb, wuv, wpa, ws, bs, wpb, wout, g2, wr, br):
    n, d = x2.shape
    tm = TM_MIX
    row = lambda w: pl.BlockSpec((tm, w), lambda i: (i, 0))
    weights = (wuv, wpa, ws, bs, wpb, wout, g2, wr, br)
    return pl.pallas_call(
        _mix_kernel,
        grid=(n // tm,),
        in_specs=[row(d), pl.BlockSpec((KV_RANK, N_HEADS * tm), lambda i: (0, i)),
                  row(GMLP_WIDTH), row(GMLP_WIDTH), row(d), row(d)]
        + [_const_spec(w.shape) for w in weights],
        out_specs=(row(d), row(d), row(LANES)),
        out_shape=(jax.ShapeDtypeStruct((n, d), F32), jax.ShapeDtypeStruct((n, d), BF16),
                   jax.ShapeDtypeStruct((n, LANES), F32)),
        compiler_params=pltpu.CompilerParams(
            dimension_semantics=("arbitrary",), vmem_limit_bytes=VMEM_LIMIT),
        name="merge_router",
    )(x2, olat_t, u, v, ga, gb, *weights)


def _moe_kernel(x1_ref, h2_ref, comb_ref, wg_ref, wu_ref, wd_ref, gf_ref, o_ref, acc_ref):
    e = pl.program_id(1)

    @pl.when(e == 0)
    def _():
        acc_ref[...] = jnp.zeros(acc_ref.shape, F32)

    h = h2_ref[...]
    comb = comb_ref[...]
    lane = lax.broadcasted_iota(jnp.int32, comb.shape, 1)
    w_e = jnp.sum(jnp.where(lane == e, comb, 0.0), axis=1, keepdims=True)
    act = jax.nn.silu(_dot(h, wg_ref[0])) * _dot(h, wu_ref[0]) * w_e
    acc_ref[...] += _dot(act.astype(BF16), wd_ref[0])

    @pl.when(e == pl.num_programs(1) - 1)
    def _():
        o_ref[...] = _rms(x1_ref[...] + acc_ref[...], gf_ref[...])


def _moe(x1, h2, comb, wg, wu, wd, gf):
    n, d = x1.shape
    tm = min(TM_MOE, n)
    row = lambda w: pl.BlockSpec((tm, w), lambda i, e: (i, 0))
    return pl.pallas_call(
        _moe_kernel,
        grid=(n // tm, N_EXPERTS),
        in_specs=[row(d), row(d), row(LANES),
                  pl.BlockSpec((1, d, D_FF_EXPERT), lambda i, e: (e, 0, 0)),
                  pl.BlockSpec((1, d, D_FF_EXPERT), lambda i, e: (e, 0, 0)),
                  pl.BlockSpec((1, D_FF_EXPERT, d), lambda i, e: (e, 0, 0)),
                  pl.BlockSpec((1, d), lambda i, e: (0, 0))],
        out_specs=row(d),
        out_shape=jax.ShapeDtypeStruct((n, d), F32),
        scratch_shapes=[pltpu.VMEM((tm, d), F32)],
        compiler_params=pltpu.CompilerParams(
            dimension_semantics=("arbitrary", "arbitrary"), vmem_limit_bytes=VMEM_LIMIT),
        name="experts",
    )(x1, h2, comb, wg, wu, wd, gf)


def _t5_bucket(n):
    max_exact = NUM_BUCKETS // 2
    nf = jnp.maximum(n, 1).astype(F32)
    large = max_exact + (jnp.log(nf / max_exact) / np.float32(np.log(MAX_DISTANCE / max_exact))
                         * (NUM_BUCKETS - max_exact)).astype(jnp.int32)
    large = jnp.minimum(large, NUM_BUCKETS - 1)
    return jnp.where(n < max_exact, n, large)


def _bias_tiles(rel_bias):
    qb = Q_BLK
    k = jnp.arange(qb, dtype=jnp.int32)[:, None]
    q = jnp.arange(qb, dtype=jnp.int32)[None, :]
    rel = rel_bias - rel_bias[NUM_BUCKETS - 1]

    def tile(dist):
        onehot = (_t5_bucket(dist)[:, :, None] == jnp.arange(NUM_BUCKETS)).astype(F32)
        t = jnp.einsum("kqb,bh->khq", onehot, rel, precision=lax.Precision.HIGHEST)
        return t.reshape(qb, N_HEADS * qb)

    return tile(jnp.maximum(q - k, 0)), tile(qb + q - k)


def _block_diag(w):
    h, a, b = w.shape
    eye = jnp.eye(h, dtype=w.dtype)
    return (eye[:, None, :, None] * w[:, :, None, :]).reshape(h * a, h * b)


def kernel(x, w_in, kv_norm_g, w_uk, w_uv, rel_bias, ln_v_g, ln_v_b, w_spatial, b_spatial,
           w_proj_a, w_proj_b, w_out, norm1_g, norm2_g, router_group_w, router_group_b,
           router_expert_w, router_expert_b, w_gate, w_up, w_down, final_norm_g):
    batch, seq, d = x.shape
    depth = w_in.shape[0]
    n = batch * seq
    assert depth == 1, "the final rms-norm is fused into the (single) layer's expert kernel"
    assert EXPERTS_PER_GROUP == 8
    assert seq % Q_BLK == 0 and n % min(TM_MOE, n) == 0
    assert TM_PROJ == Q_BLK == TM_MIX
    assert Q_BLK + 1 >= MAX_DISTANCE

    attn_w = N_HEADS * HEAD_DIM
    sizes = (attn_w, KV_RANK, IDX_HEADS * IDX_DIM, IDX_DIM, IDX_HEADS, GMLP_WIDTH, GMLP_WIDTH, d, d)
    offs = np.concatenate([[0], np.cumsum(sizes)])
    d0, d1 = _bias_tiles(rel_bias)
    tril = jnp.tril(jnp.ones((CHUNK, CHUNK), dtype=bool))
    row = lambda a: a.reshape(1, -1)

    x2 = x.reshape(n, d)
    l = 0
    wl = w_in[l].astype(BF16)
    wq, wc, wqi, wki, wwi, wu, wv, wga, wgb = [wl[:, offs[i]:offs[i + 1]] for i in range(9)]
    wt = jnp.concatenate([wq, wqi, wc, wwi], axis=1).T
    wuk_bd = _block_diag(jnp.transpose(w_uk[l], (1, 0, 2))).astype(BF16)
    wuv_t = jnp.transpose(w_uv[l], (1, 2, 0)).astype(BF16)

    qlat_t, qi_t, wi_t, ckv_t, ckv, ki, u, v, ga, gb = _projection(
        x2, row(norm1_g[l]), wt, wuk_bd, kv_norm_g[l].reshape(-1, 1), wc, row(kv_norm_g[l]),
        wki, wu, wv, row(ln_v_g[l]), row(ln_v_b[l]), wga, wgb)

    olat_t = _attention(qlat_t, qi_t, wi_t, ckv, ckv_t, ki, d0, d1, batch, seq)

    ws = jnp.where(tril[None], w_spatial[l], 0.0).astype(BF16)
    bs = jnp.broadcast_to(b_spatial[l][:, :, None], (GMLP_GROUPS, CHUNK, GMLP_GROUP_DIM))
    wr = jnp.concatenate(
        [router_expert_w[l], router_group_w[l],
         jnp.zeros((d, LANES - N_EXPERTS - N_GROUPS), F32)], axis=1)
    br = jnp.concatenate(
        [router_expert_b[l], router_group_b[l],
         jnp.zeros((LANES - N_EXPERTS - N_GROUPS,), F32)]).reshape(1, LANES)
    x1, h2, comb = _mix(
        x2, olat_t, u, v, ga, gb, wuv_t, w_proj_a[l].astype(BF16), ws, bs,
        w_proj_b[l].astype(BF16), w_out[l].astype(BF16), row(norm2_g[l]), wr, br)

    out = _moe(x1, h2, comb, w_gate[l].astype(BF16), w_up[l].astype(BF16),
               w_down[l].astype(BF16), row(final_norm_g))
    return out.reshape(batch, seq, d)
```

```python
import functools

import numpy as np
import jax
import jax.numpy as jnp
from jax import lax
from jax.experimental import pallas as pl
from jax.experimental.pallas import tpu as pltpu

F32 = jnp.float32
BF16 = jnp.bfloat16

N_HEADS = 8
HEAD_DIM = 64
KV_RANK = 128
IDX_HEADS = 8
IDX_DIM = 64
TOPK_MAX = 256
CHUNK = 128
GMLP_GROUPS = 4
GMLP_GROUP_DIM = 128
GMLP_WIDTH = GMLP_GROUPS * GMLP_GROUP_DIM
NUM_BUCKETS = 32
MAX_DISTANCE = 128
N_GROUPS = 4
EXPERTS_PER_GROUP = 8
N_EXPERTS = N_GROUPS * EXPERTS_PER_GROUP
D_FF_EXPERT = 256
EPS = 1e-6

LANES = 128
SUBLANES = 8
VMEM_LIMIT = 52 * 1024 * 1024
NEG = -3.0e38
MASK = -1.0e30

TM_PROJ = 256
Q_BLK = 256
TM_MIX = 256
TM_MOE = 1024
MOE_ROWS = 128
MOE_PERM_ROWS = 256
MOE_EXPERTS_PER_STEP = 4
SEARCH_MAX_ITERS = 64
VERIFY_FROM = 23
VERIFY_EVERY = 8


def _dot(a, b):
    return jnp.dot(a, b, preferred_element_type=F32)


def _dot_t(a, b):
    return lax.dot_general(a, b, (((1,), (1,)), ((), ())), preferred_element_type=F32)


def _rms(x, g):
    return x * lax.rsqrt(jnp.mean(x * x, axis=-1, keepdims=True) + EPS) * g


_T_ROWS = (N_HEADS * HEAD_DIM, IDX_HEADS * IDX_DIM, KV_RANK, IDX_HEADS)
_T_OFFS = tuple(int(v) for v in np.concatenate([[0], np.cumsum(_T_ROWS)]))


def _proj_kernel(x_ref, g1_ref, wt_ref, wuk_ref, gkvc_ref, wc_ref, gkv_ref, wki_ref, wu_ref,
                 wv_ref, lng_ref, lnb_ref, wga_ref, wgb_ref,
                 qlat_ref, qi_ref, wi_ref, ckvt_ref, ckv_ref, ki_ref, u_ref, v_ref, ga_ref, gb_ref):
    hb = _rms(x_ref[...], g1_ref[...]).astype(BF16)
    t_all = _dot_t(wt_ref[...], hb)
    q_t, qi_t, c_t, w_t = [t_all[_T_OFFS[i]:_T_OFFS[i + 1]] for i in range(4)]
    qlat_t = (_dot(wuk_ref[...], q_t.astype(BF16)) * (HEAD_DIM ** -0.5)).astype(BF16)
    w_t = w_t * ((IDX_HEADS ** -0.5) * (IDX_DIM ** -0.5))
    tm = x_ref.shape[0]
    for h in range(N_HEADS):
        cols = slice(h * tm, (h + 1) * tm)
        qlat_ref[:, cols] = qlat_t[h * KV_RANK:(h + 1) * KV_RANK]
        qi_ref[:, cols] = qi_t[h * IDX_DIM:(h + 1) * IDX_DIM].astype(BF16)
        wi_ref[:, cols] = w_t[h:h + 1]
    c_n = c_t * lax.rsqrt(jnp.mean(c_t * c_t, axis=0, keepdims=True) + EPS) * gkvc_ref[...]
    ckvt_ref[...] = c_n.astype(BF16)
    ckv_ref[...] = _rms(_dot(hb, wc_ref[...]), gkv_ref[...]).astype(BF16)
    ki_ref[...] = _dot(hb, wki_ref[...]).astype(BF16)
    u_ref[...] = jax.nn.gelu(_dot(hb, wu_ref[...])).astype(BF16)
    v = jax.nn.gelu(_dot(hb, wv_ref[...]))
    mu = jnp.mean(v, axis=-1, keepdims=True)
    var = jnp.mean(jnp.square(v - mu), axis=-1, keepdims=True)
    v_ref[...] = ((v - mu) * lax.rsqrt(var + EPS) * lng_ref[...] + lnb_ref[...]).astype(BF16)
    ga_ref[...] = jax.nn.sigmoid(_dot(hb, wga_ref[...])).astype(BF16)
    gb_ref[...] = jax.nn.sigmoid(_dot(hb, wgb_ref[...])).astype(BF16)


def _const_spec(shape):
    nd = len(shape)
    return pl.BlockSpec(shape, lambda *_: (0,) * nd)


def _projection(x2, g1, wt, wuk, gkvc, wc, gkv, wki, wu, wv, lng, lnb, wga, wgb):
    n, d = x2.shape
    tm = TM_PROJ
    row = lambda w: pl.BlockSpec((tm, w), lambda i: (i, 0))
    col = lambda h: pl.BlockSpec((h, tm), lambda i: (0, i))
    wide = lambda h: pl.BlockSpec((h, N_HEADS * tm), lambda i: (0, i))
    weights = (g1, wt, wuk, gkvc, wc, gkv, wki, wu, wv, lng, lnb, wga, wgb)
    out_shape = (
        jax.ShapeDtypeStruct((KV_RANK, N_HEADS * n), BF16),
        jax.ShapeDtypeStruct((IDX_DIM, IDX_HEADS * n), BF16),
        jax.ShapeDtypeStruct((1, IDX_HEADS * n), F32),
        jax.ShapeDtypeStruct((KV_RANK, n), BF16),
        jax.ShapeDtypeStruct((n, KV_RANK), BF16),
        jax.ShapeDtypeStruct((n, IDX_DIM), BF16),
        jax.ShapeDtypeStruct((n, GMLP_WIDTH), BF16),
        jax.ShapeDtypeStruct((n, GMLP_WIDTH), BF16),
        jax.ShapeDtypeStruct((n, d), BF16),
        jax.ShapeDtypeStruct((n, d), BF16),
    )
    out_specs = (
        wide(KV_RANK), wide(IDX_DIM), wide(1), col(KV_RANK),
        row(KV_RANK), row(IDX_DIM), row(GMLP_WIDTH), row(GMLP_WIDTH), row(d), row(d),
    )
    return pl.pallas_call(
        _proj_kernel,
        grid=(n // tm,),
        in_specs=[row(d)] + [_const_spec(w.shape) for w in weights],
        out_specs=out_specs,
        out_shape=out_shape,
        compiler_params=pltpu.CompilerParams(
            dimension_semantics=("arbitrary",), vmem_limit_bytes=VMEM_LIMIT),
        name="in_projection",
    )(x2, *weights)


def _attn_kernel(qlat_ref, qi_ref, wi_ref, ckv_ref, ckvt_ref, ki_ref, d0_ref, d1_ref, o_ref,
                 idx_ref, m_ref, l_ref, acc_ref, *, topk, seq):
    qb = Q_BLK
    ns = qb // SUBLANES
    j_q = pl.program_id(1)
    q0 = j_q * qb
    nch = j_q + 1
    kf = float(topk)

    def chunk_rows(j):
        return pl.ds(pl.multiple_of(j * qb, qb), qb)

    krow = lax.broadcasted_iota(jnp.int32, (qb, qb), 0)
    qcol = lax.broadcasted_iota(jnp.int32, (qb, qb), 1)

    def idx_chunk(j, carry):
        rmax, rmin = carry
        zw = jnp.maximum(_dot(ki_ref[chunk_rows(j), :], qi_ref[...]), 0.0) * wi_ref[...]
        index = zw[:, 0:qb]
        for h in range(1, IDX_HEADS):
            index = index + zw[:, h * qb:(h + 1) * qb]
        valid = (j * qb + krow) <= (q0 + qcol)
        x = jnp.where(valid, index, NEG)
        idx_ref[chunk_rows(j), :] = x
        rmax = jnp.maximum(rmax, jnp.max(x, axis=0, keepdims=True))
        rmin = jnp.minimum(rmin, jnp.min(jnp.where(valid, index, -NEG), axis=0, keepdims=True))
        return rmax, rmin

    rmax, rmin = lax.fori_loop(
        0, nch, idx_chunk, (jnp.full((1, qb), NEG, F32), jnp.full((1, qb), -NEG, F32)))

    def slabs(j):
        return idx_ref[chunk_rows(j), :].reshape(ns, SUBLANES, qb)

    def col_sum(c8):
        return jnp.sum(c8, axis=0, keepdims=True)

    def count_ge(thr):
        thr8 = jnp.broadcast_to(thr, (SUBLANES, qb))[None]

        def body(j, c8):
            return c8 + jnp.sum(jnp.where(slabs(j) >= thr8, 1.0, 0.0), axis=0)

        return col_sum(lax.fori_loop(0, nch, body, jnp.zeros((SUBLANES, qb), F32)))

    def bracket_is_tie(lo, hi):
        lo8 = jnp.broadcast_to(lo, (SUBLANES, qb))[None]
        hi8 = jnp.broadcast_to(hi, (SUBLANES, qb))[None]

        def body(j, carry):
            bmax, bmin = carry
            xs = slabs(j)
            inb = (xs >= lo8) & (xs < hi8)
            bmax = jnp.maximum(bmax, jnp.max(jnp.where(inb, xs, NEG), axis=0))
            bmin = jnp.minimum(bmin, jnp.min(jnp.where(inb, xs, -NEG), axis=0))
            return bmax, bmin

        bmax, bmin = lax.fori_loop(
            0, nch, body,
            (jnp.full((SUBLANES, qb), NEG, F32), jnp.full((SUBLANES, qb), -NEG, F32)))
        bmax = jnp.max(bmax, axis=0, keepdims=True)
        bmin = jnp.min(bmin, axis=0, keepdims=True)
        return jnp.where(bmax == bmin, 1.0, 0.0)

    nvalid = (q0 + 1 + lax.broadcasted_iota(jnp.int32, (1, qb), 1)).astype(F32)
    hi0 = rmax + (jnp.abs(rmax) * 1e-6 + 1e-30)

    def open_rows(c_lo, tie):
        return jnp.max(jnp.where((c_lo > kf) & (tie < 0.5), 1.0, 0.0)) > 0.0

    def search_cond(state):
        it, _, _, c_lo, _, tie = state
        return (it < SEARCH_MAX_ITERS) & open_rows(c_lo, tie)

    def search_body(state):
        it, lo, hi, c_lo, c_hi, tie = state
        mid = 0.5 * (lo + hi)
        c = count_ge(mid)
        ok = c >= kf
        lo, hi = jnp.where(ok, mid, lo), jnp.where(ok, hi, mid)
        c_lo, c_hi = jnp.where(ok, c, c_lo), jnp.where(ok, c_hi, c)
        check = (it >= VERIFY_FROM) & (((it - VERIFY_FROM) & (VERIFY_EVERY - 1)) == 0)
        tie = lax.cond(check, lambda: bracket_is_tie(lo, hi), lambda: tie)
        return it + 1, lo, hi, c_lo, c_hi, tie

    _, lo, hi, c_lo, c_hi, _ = lax.while_loop(
        search_cond, search_body,
        (jnp.int32(0), rmin, hi0, nvalid, jnp.zeros((1, qb), F32), jnp.zeros((1, qb), F32)))

    excess = c_lo > kf

    @pl.when(jnp.max(jnp.where(excess, 1.0, 0.0)) > 0.0)
    def _():
        need = kf - c_hi
        lo8 = jnp.broadcast_to(lo, (SUBLANES, qb))[None]
        hi8 = jnp.broadcast_to(hi, (SUBLANES, qb))[None]
        kpos = (lax.broadcasted_iota(jnp.int32, (ns, SUBLANES, qb), 0) * SUBLANES
                + lax.broadcasted_iota(jnp.int32, (ns, SUBLANES, qb), 1))

        def count_le(pos):
            pos8 = jnp.broadcast_to(pos, (SUBLANES, qb))[None]

            def body(j, c8):
                xs = slabs(j)
                inb = (xs >= lo8) & (xs < hi8) & ((j * qb + kpos) <= pos8)
                return c8 + jnp.sum(jnp.where(inb, 1.0, 0.0), axis=0)

            return col_sum(lax.fori_loop(0, nch, body, jnp.zeros((SUBLANES, qb), F32)))

        def pbisect(_, carry):
            plo, phi = carry
            pm = (plo + phi) >> 1
            ok = count_le(pm) >= need
            return jnp.where(ok, plo, pm), jnp.where(ok, pm, phi)

        nbits = int(np.ceil(np.log2(seq + 1)))
        _, pcut = lax.fori_loop(
            0, nbits, pbisect,
            (jnp.full((1, qb), -1, jnp.int32), jnp.full((1, qb), seq - 1, jnp.int32)))
        pcut8 = jnp.broadcast_to(jnp.where(excess, pcut, seq), (SUBLANES, qb))[None]

        def demote(j, carry):
            xs = slabs(j)
            drop = (xs >= lo8) & (xs < hi8) & ((j * qb + kpos) > pcut8)
            idx_ref[chunk_rows(j), :] = jnp.where(drop, NEG, xs).reshape(qb, qb)
            return carry

        lax.fori_loop(0, nch, demote, 0)

    m_ref[...] = jnp.full(m_ref.shape, MASK, F32)
    l_ref[...] = jnp.zeros(l_ref.shape, F32)
    acc_ref[...] = jnp.zeros(acc_ref.shape, F32)

    def att_step(j, bias_ref):
        ckc = ckv_ref[chunk_rows(j), :]
        ckc_t = ckvt_ref[:, chunk_rows(j)]
        madd = jnp.where(idx_ref[chunk_rows(j), :] >= lo, 0.0, MASK)
        s_all = _dot(ckc, qlat_ref[...])
        s_h = []
        for h in range(N_HEADS):
            s = s_all[:, h * qb:(h + 1) * qb] + madd
            if bias_ref is not None:
                s = s + bias_ref[:, h * qb:(h + 1) * qb]
            s_h.append(s)
        s_all = jnp.concatenate(s_h, axis=1)
        m_old = m_ref[...]
        m_new = jnp.maximum(m_old, jnp.max(s_all, axis=0, keepdims=True))
        alpha = jnp.exp(m_old - m_new)
        p = jnp.exp(s_all - m_new)
        l_ref[...] = alpha * l_ref[...] + jnp.sum(p, axis=0, keepdims=True)
        acc_ref[...] = alpha * acc_ref[...] + _dot(ckc_t, p.astype(BF16))
        m_ref[...] = m_new

    def far_step(j, carry):
        att_step(j, None)
        return carry

    lax.fori_loop(0, jnp.maximum(nch - 2, 0), far_step, 0)

    @pl.when(nch >= 2)
    def _():
        att_step(nch - 2, d1_ref)

    att_step(nch - 1, d0_ref)

    o_ref[...] = (acc_ref[...] / l_ref[...]).astype(BF16)


def _attention(qlat_w, qi_w, wi_w, ckv, ckv_t, ki, d0, d1, batch, seq):
    qb = Q_BLK
    nq = seq // qb
    n = batch * seq
    topk = min(TOPK_MAX, seq // 4)
    kern = functools.partial(_attn_kernel, topk=topk, seq=seq)
    wide = lambda h: pl.BlockSpec((h, N_HEADS * qb), lambda b, j: (0, b * nq + j))
    return pl.pallas_call(
        kern,
        grid=(batch, nq),
        in_specs=[
            wide(KV_RANK), wide(IDX_DIM), wide(1),
            pl.BlockSpec((seq, KV_RANK), lambda b, j: (b, 0)),
            pl.BlockSpec((KV_RANK, seq), lambda b, j: (0, b)),
            pl.BlockSpec((seq, IDX_DIM), lambda b, j: (b, 0)),
            _const_spec(d0.shape), _const_spec(d1.shape),
        ],
        out_specs=wide(KV_RANK),
        out_shape=jax.ShapeDtypeStruct((KV_RANK, N_HEADS * n), BF16),
        scratch_shapes=[
            pltpu.VMEM((seq, qb), F32),
            pltpu.VMEM((1, N_HEADS * qb), F32),
            pltpu.VMEM((1, N_HEADS * qb), F32),
            pltpu.VMEM((KV_RANK, N_HEADS * qb), F32),
        ],
        compiler_params=pltpu.CompilerParams(
            dimension_semantics=("arbitrary", "arbitrary"), vmem_limit_bytes=VMEM_LIMIT),
        name="sparse_attention",
    )(qlat_w, qi_w, wi_w, ckv, ckv_t, ki, d0, d1)


def _mix_kernel(x_ref, ol_ref, u_ref, v_ref, ga_ref, gb_ref, wuv_ref, wpa_ref, ws_ref, bs_ref,
                wpb_ref, wout_ref, g2_ref, wr_ref, br_ref, x1_ref, h2_ref, comb_ref, goh_ref):
    tm = x_ref.shape[0]
    o_at = jnp.concatenate(
        [_dot(wuv_ref[h], ol_ref[:, h * tm:(h + 1) * tm]) for h in range(N_HEADS)], axis=0)
    y_a = _dot(jnp.transpose(o_at).astype(BF16), wpa_ref[...])
    v = v_ref[...]
    u = u_ref[...].astype(F32)
    rows = []
    for c in range(tm // CHUNK):
        cols = []
        for g in range(GMLP_GROUPS):
            vc = v[c * CHUNK:(c + 1) * CHUNK, g * GMLP_GROUP_DIM:(g + 1) * GMLP_GROUP_DIM]
            cols.append(_dot(ws_ref[g], vc) + bs_ref[g])
        rows.append(jnp.concatenate(cols, axis=1))
    s = jnp.concatenate(rows, axis=0)
    y_b = _dot((u * s).astype(BF16), wpb_ref[...])
    merged = ga_ref[...].astype(F32) * y_a + gb_ref[...].astype(F32) * y_b
    x1 = x_ref[...] + _dot(merged.astype(BF16), wout_ref[...])
    x1_ref[...] = x1
    h2 = _rms(x1, g2_ref[...])
    h2_ref[...] = h2.astype(BF16)

    logits = jnp.dot(h2, wr_ref[...], preferred_element_type=F32,
                     precision=lax.Precision.HIGHEST) + br_ref[...]
    lane = lax.broadcasted_iota(jnp.int32, logits.shape, 1)
    big = jnp.int32(4 * LANES)
    is_g = (lane >= N_EXPERTS) & (lane < N_EXPERTS + N_GROUPS)
    lg = jnp.where(is_g, logits, NEG)
    gmax = jnp.max(lg, axis=1, keepdims=True)
    g_w = 1.0 / jnp.sum(jnp.where(is_g, jnp.exp(lg - gmax), 0.0), axis=1, keepdims=True)
    g_idx = jnp.min(jnp.where(is_g & (lg == gmax), lane, big), axis=1, keepdims=True) - N_EXPERTS
    in_g = (lane < N_EXPERTS) & ((lane >> 3) == g_idx)
    le = jnp.where(in_g, logits, NEG)
    emax = jnp.max(le, axis=1, keepdims=True)
    ee = jnp.where(in_g, jnp.exp(le - emax), 0.0)
    ep = ee / jnp.sum(ee, axis=1, keepdims=True)
    ep1 = jnp.where(in_g, ep, -1.0)
    m1 = jnp.max(ep1, axis=1, keepdims=True)
    i1 = jnp.min(jnp.where(ep1 == m1, lane, big), axis=1, keepdims=True)
    ep2 = jnp.where(lane == i1, -1.0, ep1)
    m2 = jnp.max(ep2, axis=1, keepdims=True)
    i2 = jnp.min(jnp.where(ep2 == m2, lane, big), axis=1, keepdims=True)
    tot = m1 + m2
    comb_ref[...] = (jnp.where(lane == i1, g_w * m1 / tot, 0.0)
                     + jnp.where(lane == i2, g_w * m2 / tot, 0.0))
    goh_ref[...] = jnp.transpose(jnp.where(lane == g_idx, 1.0, 0.0))[0:SUBLANES, :]


def _mix(x2, olat_t, u, v, ga, gb, wuv, wpa, ws, bs, wpb, wout, g2, wr, br):
    n, d = x2.shape
    tm = TM_MIX
    row = lambda w: pl.BlockSpec((tm, w), lambda i: (i, 0))
    weights = (wuv, wpa, ws, bs, wpb, wout, g2, wr, br)
    return pl.pallas_call(
        _mix_kernel,
        grid=(n // tm,),
        in_specs=[row(d), pl.BlockSpec((KV_RANK, N_HEADS * tm), lambda i: (0, i)),
                  row(GMLP_WIDTH), row(GMLP_WIDTH), row(d), row(d)]
        + [_const_spec(w.shape) for w in weights],
        out_specs=(row(d), row(d), row(LANES), pl.BlockSpec((SUBLANES, tm), lambda i: (0, i))),
        out_shape=(jax.ShapeDtypeStruct((n, d), F32), jax.ShapeDtypeStruct((n, d), BF16),
                   jax.ShapeDtypeStruct((n, LANES), F32),
                   jax.ShapeDtypeStruct((SUBLANES, n), F32)),
        compiler_params=pltpu.CompilerParams(
            dimension_semantics=("arbitrary",), vmem_limit_bytes=VMEM_LIMIT),
        name="merge_router",
    )(x2, olat_t, u, v, ga, gb, *weights)


def _moe_kernel(x1_ref, h2_ref, comb_ref, goh_ref, tri_ref, wg_ref, wu_ref, wd_ref, gf_ref, o_ref,
                perm_ref, pos_ref, hs_ref, cs_ref, ys_ref, seg_ref):
    step = pl.program_id(1)
    tt, d = hs_ref.shape
    rb = MOE_ROWS
    pc = MOE_PERM_ROWS

    @pl.when(step == 0)
    def _():
        goh = goh_ref[...]
        before = _dot(goh.astype(BF16), tri_ref[...])
        pos = jnp.zeros((1, tt), F32)
        start = jnp.float32(0.0)
        for g in range(N_GROUPS):
            cnt = jnp.sum(goh[g:g + 1, :])
            seg_ref[g] = start.astype(jnp.int32)
            seg_ref[N_GROUPS + g] = (start + cnt).astype(jnp.int32)
            pos = pos + goh[g:g + 1, :] * (start + before[g:g + 1, :])
            start = start + cnt
        pos_ref[...] = pos
        for r in range(tt // pc):
            dst = (lax.broadcasted_iota(jnp.int32, (pc, tt), 0) + r * pc).astype(F32)
            perm_ref[r * pc:(r + 1) * pc, :] = jnp.where(pos == dst, 1.0, 0.0).astype(BF16)
        perm = perm_ref[...]
        hs_ref[...] = _dot(perm, h2_ref[...]).astype(BF16)
        comb = comb_ref[...]
        comb_hi = comb.astype(BF16)
        comb_lo = (comb - comb_hi.astype(F32)).astype(BF16)
        cs_ref[...] = _dot(perm, comb_hi) + _dot(perm, comb_lo)
        ys_ref[...] = jnp.zeros(ys_ref.shape, F32)

    group = step // (EXPERTS_PER_GROUP // MOE_EXPERTS_PER_STEP)
    first = seg_ref[group] // rb
    last = (seg_ref[N_GROUPS + group] + rb - 1) // rb

    def block(b, carry):
        rows = pl.ds(pl.multiple_of(b * rb, rb), rb)
        hb = hs_ref[rows, :]
        cb = cs_ref[rows, :]
        lane = lax.broadcasted_iota(jnp.int32, cb.shape, 1)
        y = jnp.zeros((rb, d), F32)
        for k in range(MOE_EXPERTS_PER_STEP):
            e = step * MOE_EXPERTS_PER_STEP + k
            w_e = jnp.sum(jnp.where(lane == e, cb, 0.0), axis=1, keepdims=True)
            act = jax.nn.silu(_dot(hb, wg_ref[k])) * _dot(hb, wu_ref[k]) * w_e
            y = y + _dot(act.astype(BF16), wd_ref[k])
        ys_ref[rows, :] += y
        return carry

    lax.fori_loop(first, last, block, 0)

    @pl.when(step == pl.num_programs(1) - 1)
    def _():
        ys = ys_ref[...].astype(BF16)
        pos_col = jnp.transpose(jnp.broadcast_to(pos_ref[...], (SUBLANES, tt)))[:, 0:1]
        for r in range(tt // pc):
            rows = slice(r * pc, (r + 1) * pc)
            src = lax.broadcasted_iota(jnp.int32, (pc, tt), 1).astype(F32)
            unperm = jnp.where(pos_col[rows] == src, 1.0, 0.0).astype(BF16)
            o_ref[rows, :] = _rms(x1_ref[rows, :] + _dot(unperm, ys), gf_ref[...])


def _moe(x1, h2, comb, goh_t, wg, wu, wd, gf):
    n, d = x1.shape
    tt = min(TM_MOE, n)
    eps = MOE_EXPERTS_PER_STEP
    assert EXPERTS_PER_GROUP % eps == 0 and tt % MOE_ROWS == 0 and tt % MOE_PERM_ROWS == 0
    row = lambda w: pl.BlockSpec((tt, w), lambda i, s: (i, 0))
    tri = (jnp.arange(tt)[:, None] < jnp.arange(tt)[None, :]).astype(BF16)
    return pl.pallas_call(
        _moe_kernel,
        grid=(n // tt, N_EXPERTS // eps),
        in_specs=[row(d), row(d), row(LANES),
                  pl.BlockSpec((SUBLANES, tt), lambda i, s: (0, i)),
                  _const_spec(tri.shape),
                  pl.BlockSpec((eps, d, D_FF_EXPERT), lambda i, s: (s, 0, 0)),
                  pl.BlockSpec((eps, d, D_FF_EXPERT), lambda i, s: (s, 0, 0)),
                  pl.BlockSpec((eps, D_FF_EXPERT, d), lambda i, s: (s, 0, 0)),
                  _const_spec(gf.shape)],
        out_specs=row(d),
        out_shape=jax.ShapeDtypeStruct((n, d), F32),
        scratch_shapes=[
            pltpu.VMEM((tt, tt), BF16),
            pltpu.VMEM((1, tt), F32),
            pltpu.VMEM((tt, d), BF16),
            pltpu.VMEM((tt, LANES), F32),
            pltpu.VMEM((tt, d), F32),
            pltpu.SMEM((2 * N_GROUPS,), jnp.int32),
        ],
        compiler_params=pltpu.CompilerParams(
            dimension_semantics=("arbitrary", "arbitrary"), vmem_limit_bytes=VMEM_LIMIT),
        name="experts",
    )(x1, h2, comb, goh_t, tri, wg, wu, wd, gf)


def _t5_bucket(n):
    max_exact = NUM_BUCKETS // 2
    nf = jnp.maximum(n, 1).astype(F32)
    large = max_exact + (jnp.log(nf / max_exact) / np.float32(np.log(MAX_DISTANCE / max_exact))
                         * (NUM_BUCKETS - max_exact)).astype(jnp.int32)
    large = jnp.minimum(large, NUM_BUCKETS - 1)
    return jnp.where(n < max_exact, n, large)


def _bias_tiles(rel_bias):
    qb = Q_BLK
    k = jnp.arange(qb, dtype=jnp.int32)[:, None]
    q = jnp.arange(qb, dtype=jnp.int32)[None, :]
    rel = rel_bias - rel_bias[NUM_BUCKETS - 1]

    def tile(dist):
        onehot = (_t5_bucket(dist)[:, :, None] == jnp.arange(NUM_BUCKETS)).astype(F32)
        t = jnp.einsum("kqb,bh->khq", onehot, rel, precision=lax.Precision.HIGHEST)
        return t.reshape(qb, N_HEADS * qb)

    return tile(jnp.maximum(q - k, 0)), tile(qb + q - k)


def _block_diag(w):
    h, a, b = w.shape
    eye = jnp.eye(h, dtype=w.dtype)
    return (eye[:, None, :, None] * w[:, :, None, :]).reshape(h * a, h * b)


def kernel(x, w_in, kv_norm_g, w_uk, w_uv, rel_bias, ln_v_g, ln_v_b, w_spatial, b_spatial,
           w_proj_a, w_proj_b, w_out, norm1_g, norm2_g, router_group_w, router_group_b,
           router_expert_w, router_expert_b, w_gate, w_up, w_down, final_norm_g):
    batch, seq, d = x.shape
    depth = w_in.shape[0]
    n = batch * seq
    assert depth == 1, "the final rms-norm is fused into the (single) layer's expert kernel"
    assert EXPERTS_PER_GROUP == 8
    assert seq % Q_BLK == 0 and n % min(TM_MOE, n) == 0
    assert TM_PROJ == Q_BLK == TM_MIX
    assert Q_BLK + 1 >= MAX_DISTANCE

    attn_w = N_HEADS * HEAD_DIM
    sizes = (attn_w, KV_RANK, IDX_HEADS * IDX_DIM, IDX_DIM, IDX_HEADS, GMLP_WIDTH, GMLP_WIDTH, d, d)
    offs = np.concatenate([[0], np.cumsum(sizes)])
    d0, d1 = _bias_tiles(rel_bias)
    tril = jnp.tril(jnp.ones((CHUNK, CHUNK), dtype=bool))
    row = lambda a: a.reshape(1, -1)

    x2 = x.reshape(n, d)
    l = 0
    wl = w_in[l].astype(BF16)
    wq, wc, wqi, wki, wwi, wu, wv, wga, wgb = [wl[:, offs[i]:offs[i + 1]] for i in range(9)]
    wt = jnp.concatenate([wq, wqi, wc, wwi], axis=1).T
    wuk_bd = _block_diag(jnp.transpose(w_uk[l], (1, 0, 2))).astype(BF16)
    wuv_t = jnp.transpose(w_uv[l], (1, 2, 0)).astype(BF16)

    qlat_t, qi_t, wi_t, ckv_t, ckv, ki, u, v, ga, gb = _projection(
        x2, row(norm1_g[l]), wt, wuk_bd, kv_norm_g[l].reshape(-1, 1), wc, row(kv_norm_g[l]),
        wki, wu, wv, row(ln_v_g[l]), row(ln_v_b[l]), wga, wgb)

    olat_t = _attention(qlat_t, qi_t, wi_t, ckv, ckv_t, ki, d0, d1, batch, seq)

    ws = jnp.where(tril[None], w_spatial[l], 0.0).astype(BF16)
    bs = jnp.broadcast_to(b_spatial[l][:, :, None], (GMLP_GROUPS, CHUNK, GMLP_GROUP_DIM))
    wr = jnp.concatenate(
        [router_expert_w[l], router_group_w[l],
         jnp.zeros((d, LANES - N_EXPERTS - N_GROUPS), F32)], axis=1)
    br = jnp.concatenate(
        [router_expert_b[l], router_group_b[l],
         jnp.zeros((LANES - N_EXPERTS - N_GROUPS,), F32)]).reshape(1, LANES)
    x1, h2, comb, goh_t = _mix(
        x2, olat_t, u, v, ga, gb, wuv_t, w_proj_a[l].astype(BF16), ws, bs,
        w_proj_b[l].astype(BF16), w_out[l].astype(BF16), row(norm2_g[l]), wr, br)

    out = _moe(x1, h2, comb, goh_t, w_gate[l].astype(BF16), w_up[l].astype(BF16),
               w_down[l].astype(BF16), row(final_norm_g))
    return out.reshape(batch, seq, d)
```

```python
import functools

import numpy as np
import jax
import jax.numpy as jnp
from jax import lax
from jax.experimental import pallas as pl
from jax.experimental.pallas import tpu as pltpu

F32 = jnp.float32
BF16 = jnp.bfloat16

N_HEADS = 8
HEAD_DIM = 64
KV_RANK = 128
IDX_HEADS = 8
IDX_DIM = 64
TOPK_MAX = 256
CHUNK = 128
GMLP_GROUPS = 4
GMLP_GROUP_DIM = 128
GMLP_WIDTH = GMLP_GROUPS * GMLP_GROUP_DIM
NUM_BUCKETS = 32
MAX_DISTANCE = 128
N_GROUPS = 4
EXPERTS_PER_GROUP = 8
N_EXPERTS = N_GROUPS * EXPERTS_PER_GROUP
D_FF_EXPERT = 256
EPS = 1e-6

LANES = 128
SUBLANES = 8
VMEM_LIMIT = 52 * 1024 * 1024
NEG = -3.0e38
MASK = -1.0e30

TM_PROJ = 256
Q_BLK = 256
TM_MIX = 256
TM_MOE = 1024
MOE_ROWS = 128
MOE_PERM_ROWS = 256
MOE_EXPERTS_PER_STEP = 4
SEARCH_MAX_ITERS = 64
SEARCH_UNROLL = 2
COUNT_ACCS = 4
VERIFY_FROM = 24
VERIFY_EVERY = 8
LOG2E = 1.4426950408889634


def _dot(a, b):
    return jnp.dot(a, b, preferred_element_type=F32)


def _dot_t(a, b):
    return lax.dot_general(a, b, (((1,), (1,)), ((), ())), preferred_element_type=F32)


def _rms(x, g):
    return x * lax.rsqrt(jnp.mean(x * x, axis=-1, keepdims=True) + EPS) * g


_T_ROWS = (N_HEADS * HEAD_DIM, IDX_HEADS * IDX_DIM, KV_RANK, IDX_HEADS)
_T_OFFS = tuple(int(v) for v in np.concatenate([[0], np.cumsum(_T_ROWS)]))


def _proj_kernel(x_ref, g1_ref, wt_ref, wuk_ref, gkvc_ref, wc_ref, gkv_ref, wki_ref, wu_ref,
                 wv_ref, lng_ref, lnb_ref, wga_ref, wgb_ref,
                 qlat_ref, qi_ref, wi_ref, ckvt_ref, ckv_ref, ki_ref, u_ref, v_ref, ga_ref, gb_ref):
    hb = _rms(x_ref[...], g1_ref[...]).astype(BF16)
    t_all = _dot_t(wt_ref[...], hb)
    q_t, qi_t, c_t, w_t = [t_all[_T_OFFS[i]:_T_OFFS[i + 1]] for i in range(4)]
    qlat_t = (_dot(wuk_ref[...], q_t.astype(BF16)) * ((HEAD_DIM ** -0.5) * LOG2E)).astype(BF16)
    w_t = w_t * ((IDX_HEADS ** -0.5) * (IDX_DIM ** -0.5))
    tm = x_ref.shape[0]
    for h in range(N_HEADS):
        cols = slice(h * tm, (h + 1) * tm)
        qlat_ref[:, cols] = qlat_t[h * KV_RANK:(h + 1) * KV_RANK]
        qi_ref[:, cols] = qi_t[h * IDX_DIM:(h + 1) * IDX_DIM].astype(BF16)
        wi_ref[:, cols] = w_t[h:h + 1]
    c_n = c_t * lax.rsqrt(jnp.mean(c_t * c_t, axis=0, keepdims=True) + EPS) * gkvc_ref[...]
    ckvt_ref[...] = c_n.astype(BF16)
    ckv_ref[...] = _rms(_dot(hb, wc_ref[...]), gkv_ref[...]).astype(BF16)
    ki_ref[...] = _dot(hb, wki_ref[...]).astype(BF16)
    u_ref[...] = jax.nn.gelu(_dot(hb, wu_ref[...])).astype(BF16)
    v = jax.nn.gelu(_dot(hb, wv_ref[...]))
    mu = jnp.mean(v, axis=-1, keepdims=True)
    var = jnp.mean(jnp.square(v - mu), axis=-1, keepdims=True)
    v_ref[...] = ((v - mu) * lax.rsqrt(var + EPS) * lng_ref[...] + lnb_ref[...]).astype(BF16)
    ga_ref[...] = jax.nn.sigmoid(_dot(hb, wga_ref[...])).astype(BF16)
    gb_ref[...] = jax.nn.sigmoid(_dot(hb, wgb_ref[...])).astype(BF16)


def _const_spec(shape):
    nd = len(shape)
    return pl.BlockSpec(shape, lambda *_: (0,) * nd)


def _projection(x2, g1, wt, wuk, gkvc, wc, gkv, wki, wu, wv, lng, lnb, wga, wgb):
    n, d = x2.shape
    tm = TM_PROJ
    row = lambda w: pl.BlockSpec((tm, w), lambda i: (i, 0))
    col = lambda h: pl.BlockSpec((h, tm), lambda i: (0, i))
    wide = lambda h: pl.BlockSpec((h, N_HEADS * tm), lambda i: (0, i))
    weights = (g1, wt, wuk, gkvc, wc, gkv, wki, wu, wv, lng, lnb, wga, wgb)
    out_shape = (
        jax.ShapeDtypeStruct((KV_RANK, N_HEADS * n), BF16),
        jax.ShapeDtypeStruct((IDX_DIM, IDX_HEADS * n), BF16),
        jax.ShapeDtypeStruct((1, IDX_HEADS * n), F32),
        jax.ShapeDtypeStruct((KV_RANK, n), BF16),
        jax.ShapeDtypeStruct((n, KV_RANK), BF16),
        jax.ShapeDtypeStruct((n, IDX_DIM), BF16),
        jax.ShapeDtypeStruct((n, GMLP_WIDTH), BF16),
        jax.ShapeDtypeStruct((n, GMLP_WIDTH), BF16),
        jax.ShapeDtypeStruct((n, d), BF16),
        jax.ShapeDtypeStruct((n, d), BF16),
    )
    out_specs = (
        wide(KV_RANK), wide(IDX_DIM), wide(1), col(KV_RANK),
        row(KV_RANK), row(IDX_DIM), row(GMLP_WIDTH), row(GMLP_WIDTH), row(d), row(d),
    )
    return pl.pallas_call(
        _proj_kernel,
        grid=(n // tm,),
        in_specs=[row(d)] + [_const_spec(w.shape) for w in weights],
        out_specs=out_specs,
        out_shape=out_shape,
        compiler_params=pltpu.CompilerParams(
            dimension_semantics=("arbitrary",), vmem_limit_bytes=VMEM_LIMIT),
        name="in_projection",
    )(x2, *weights)


def _attn_kernel(qlat_ref, qi_ref, wi_ref, ckv_ref, ckvt_ref, ki_ref, d0_ref, d1_ref, o_ref,
                 idx_ref, m_ref, l_ref, acc_ref, *, topk, seq):
    qb = Q_BLK
    ns = qb // SUBLANES
    j_q = pl.program_id(1)
    q0 = j_q * qb
    nch = j_q + 1
    kf = float(topk)

    def chunk_rows(j):
        return pl.ds(pl.multiple_of(j * qb, qb), qb)

    krow = lax.broadcasted_iota(jnp.int32, (qb, qb), 0)
    qcol = lax.broadcasted_iota(jnp.int32, (qb, qb), 1)

    def idx_chunk(j, carry):
        rmax, rmin = carry
        zw = jnp.maximum(_dot(ki_ref[chunk_rows(j), :], qi_ref[...]), 0.0) * wi_ref[...]
        index = zw[:, 0:qb]
        for h in range(1, IDX_HEADS):
            index = index + zw[:, h * qb:(h + 1) * qb]
        valid = (j * qb + krow) <= (q0 + qcol)
        x = jnp.where(valid, index, NEG)
        idx_ref[chunk_rows(j), :] = x
        rmax = jnp.maximum(rmax, jnp.max(x, axis=0, keepdims=True))
        rmin = jnp.minimum(rmin, jnp.min(jnp.where(valid, index, -NEG), axis=0, keepdims=True))
        return rmax, rmin

    rmax, rmin = lax.fori_loop(
        0, nch, idx_chunk, (jnp.full((1, qb), NEG, F32), jnp.full((1, qb), -NEG, F32)))

    @pl.when((nch & 1) == 1)
    def _():
        idx_ref[chunk_rows(nch), :] = jnp.full((qb, qb), NEG, F32)

    def slabs(j):
        return idx_ref[chunk_rows(j), :].reshape(ns, SUBLANES, qb)

    def col_sum(c8):
        return jnp.sum(c8, axis=0, keepdims=True)

    def count_ge(thr):
        thr8 = jnp.broadcast_to(thr, (SUBLANES, qb))

        def body(jp, accs):
            pair = idx_ref[pl.ds(pl.multiple_of(jp * (2 * qb), 2 * qb), 2 * qb), :]
            accs = list(accs)
            for s in range(2 * ns):
                xs = pair[s * SUBLANES:(s + 1) * SUBLANES]
                accs[s % COUNT_ACCS] = accs[s % COUNT_ACCS] + jnp.where(xs >= thr8, 1.0, 0.0)
            return tuple(accs)

        zero = jnp.zeros((SUBLANES, qb), F32)
        accs = lax.fori_loop(0, (nch + 1) >> 1, body, (zero,) * COUNT_ACCS)
        return col_sum(functools.reduce(lambda a, b: a + b, accs))

    def bracket_is_tie(lo, hi):
        lo8 = jnp.broadcast_to(lo, (SUBLANES, qb))[None]
        hi8 = jnp.broadcast_to(hi, (SUBLANES, qb))[None]

        def body(j, carry):
            bmax, bmin = carry
            xs = slabs(j)
            inb = (xs >= lo8) & (xs < hi8)
            bmax = jnp.maximum(bmax, jnp.max(jnp.where(inb, xs, NEG), axis=0))
            bmin = jnp.minimum(bmin, jnp.min(jnp.where(inb, xs, -NEG), axis=0))
            return bmax, bmin

        bmax, bmin = lax.fori_loop(
            0, nch, body,
            (jnp.full((SUBLANES, qb), NEG, F32), jnp.full((SUBLANES, qb), -NEG, F32)))
        bmax = jnp.max(bmax, axis=0, keepdims=True)
        bmin = jnp.min(bmin, axis=0, keepdims=True)
        return jnp.where(bmax == bmin, 1.0, 0.0)

    nvalid = (q0 + 1 + lax.broadcasted_iota(jnp.int32, (1, qb), 1)).astype(F32)
    hi0 = rmax + (jnp.abs(rmax) * 1e-6 + 1e-30)

    def open_rows(c_lo, tie):
        return jnp.max(jnp.where((c_lo > kf) & (tie < 0.5), 1.0, 0.0)) > 0.0

    def search_cond(state):
        it, _, _, c_lo, _, tie = state
        return (it < SEARCH_MAX_ITERS) & open_rows(c_lo, tie)

    def search_body(state):
        it, lo, hi, c_lo, c_hi, tie = state
        for _ in range(SEARCH_UNROLL):
            mid = 0.5 * (lo + hi)
            c = count_ge(mid)
            ok = c >= kf
            lo, hi = jnp.where(ok, mid, lo), jnp.where(ok, hi, mid)
            c_lo, c_hi = jnp.where(ok, c, c_lo), jnp.where(ok, c_hi, c)
        it = it + SEARCH_UNROLL
        check = (it >= VERIFY_FROM) & (((it - VERIFY_FROM) & (VERIFY_EVERY - 1)) == 0)
        tie = lax.cond(check, lambda: bracket_is_tie(lo, hi), lambda: tie)
        return it, lo, hi, c_lo, c_hi, tie

    _, lo, hi, c_lo, c_hi, _ = lax.while_loop(
        search_cond, search_body,
        (jnp.int32(0), rmin, hi0, nvalid, jnp.zeros((1, qb), F32), jnp.zeros((1, qb), F32)))

    excess = c_lo > kf

    @pl.when(jnp.max(jnp.where(excess, 1.0, 0.0)) > 0.0)
    def _():
        need = kf - c_hi
        lo8 = jnp.broadcast_to(lo, (SUBLANES, qb))[None]
        hi8 = jnp.broadcast_to(hi, (SUBLANES, qb))[None]
        kpos = (lax.broadcasted_iota(jnp.int32, (ns, SUBLANES, qb), 0) * SUBLANES
                + lax.broadcasted_iota(jnp.int32, (ns, SUBLANES, qb), 1))

        def count_le(pos):
            pos8 = jnp.broadcast_to(pos, (SUBLANES, qb))[None]

            def body(j, c8):
                xs = slabs(j)
                inb = (xs >= lo8) & (xs < hi8) & ((j * qb + kpos) <= pos8)
                return c8 + jnp.sum(jnp.where(inb, 1.0, 0.0), axis=0)

            return col_sum(lax.fori_loop(0, nch, body, jnp.zeros((SUBLANES, qb), F32)))

        def pbisect(_, carry):
            plo, phi = carry
            pm = (plo + phi) >> 1
            ok = count_le(pm) >= need
            return jnp.where(ok, plo, pm), jnp.where(ok, pm, phi)

        nbits = int(np.ceil(np.log2(seq + 1)))
        _, pcut = lax.fori_loop(
            0, nbits, pbisect,
            (jnp.full((1, qb), -1, jnp.int32), jnp.full((1, qb), seq - 1, jnp.int32)))
        pcut8 = jnp.broadcast_to(jnp.where(excess, pcut, seq), (SUBLANES, qb))[None]

        def demote(j, carry):
            xs = slabs(j)
            drop = (xs >= lo8) & (xs < hi8) & ((j * qb + kpos) > pcut8)
            idx_ref[chunk_rows(j), :] = jnp.where(drop, NEG, xs).reshape(qb, qb)
            return carry

        lax.fori_loop(0, nch, demote, 0)

    m_ref[...] = jnp.full(m_ref.shape, MASK, F32)
    l_ref[...] = jnp.zeros(l_ref.shape, F32)
    acc_ref[...] = jnp.zeros(acc_ref.shape, F32)

    def att_step(j, bias_ref):
        ckc = ckv_ref[chunk_rows(j), :]
        ckc_t = ckvt_ref[:, chunk_rows(j)]
        madd = jnp.where(idx_ref[chunk_rows(j), :] >= lo, 0.0, MASK)
        s_all = _dot(ckc, qlat_ref[...])
        s_h = []
        for h in range(N_HEADS):
            s = s_all[:, h * qb:(h + 1) * qb] + madd
            if bias_ref is not None:
                s = s + bias_ref[:, h * qb:(h + 1) * qb]
            s_h.append(s)
        s_all = jnp.concatenate(s_h, axis=1)
        m_old = m_ref[...]
        m_new = jnp.maximum(m_old, jnp.max(s_all, axis=0, keepdims=True))
        alpha = jnp.exp2(m_old - m_new)
        p = jnp.exp2(s_all - m_new)
        l_ref[...] = alpha * l_ref[...] + jnp.sum(p, axis=0, keepdims=True)
        acc_ref[...] = alpha * acc_ref[...] + _dot(ckc_t, p.astype(BF16))
        m_ref[...] = m_new

    def far_step(j, carry):
        att_step(j, None)
        return carry

    lax.fori_loop(0, jnp.maximum(nch - 2, 0), far_step, 0)

    @pl.when(nch >= 2)
    def _():
        att_step(nch - 2, d1_ref)

    att_step(nch - 1, d0_ref)

    o_ref[...] = (acc_ref[...] / l_ref[...]).astype(BF16)


def _attention(qlat_w, qi_w, wi_w, ckv, ckv_t, ki, d0, d1, batch, seq):
    qb = Q_BLK
    nq = seq // qb
    n = batch * seq
    topk = min(TOPK_MAX, seq // 4)
    kern = functools.partial(_attn_kernel, topk=topk, seq=seq)
    wide = lambda h: pl.BlockSpec((h, N_HEADS * qb), lambda b, j: (0, b * nq + j))
    return pl.pallas_call(
        kern,
        grid=(batch, nq),
        in_specs=[
            wide(KV_RANK), wide(IDX_DIM), wide(1),
            pl.BlockSpec((seq, KV_RANK), lambda b, j: (b, 0)),
            pl.BlockSpec((KV_RANK, seq), lambda b, j: (0, b)),
            pl.BlockSpec((seq, IDX_DIM), lambda b, j: (b, 0)),
            _const_spec(d0.shape), _const_spec(d1.shape),
        ],
        out_specs=wide(KV_RANK),
        out_shape=jax.ShapeDtypeStruct((KV_RANK, N_HEADS * n), BF16),
        scratch_shapes=[
            pltpu.VMEM((seq + qb, qb), F32),
            pltpu.VMEM((1, N_HEADS * qb), F32),
            pltpu.VMEM((1, N_HEADS * qb), F32),
            pltpu.VMEM((KV_RANK, N_HEADS * qb), F32),
        ],
        compiler_params=pltpu.CompilerParams(
            dimension_semantics=("arbitrary", "arbitrary"), vmem_limit_bytes=VMEM_LIMIT),
        name="sparse_attention",
    )(qlat_w, qi_w, wi_w, ckv, ckv_t, ki, d0, d1)


def _mix_kernel(x_ref, ol_ref, u_ref, v_ref, ga_ref, gb_ref, wuv_ref, wpa_ref, ws_ref, bs_ref,
                wpb_ref, wout_ref, g2_ref, wr_ref, br_ref, x1_ref, h2_ref, comb_ref, goh_ref):
    tm = x_ref.shape[0]
    o_at = jnp.concatenate(
        [_dot(wuv_ref[h], ol_ref[:, h * tm:(h + 1) * tm]) for h in range(N_HEADS)], axis=0)
    y_a = _dot(jnp.transpose(o_at).astype(BF16), wpa_ref[...])
    v = v_ref[...]
    u = u_ref[...].astype(F32)
    rows = []
    for c in range(tm // CHUNK):
        cols = []
        for g in range(GMLP_GROUPS):
            vc = v[c * CHUNK:(c + 1) * CHUNK, g * GMLP_GROUP_DIM:(g + 1) * GMLP_GROUP_DIM]
            cols.append(_dot(ws_ref[g], vc) + bs_ref[g])
        rows.append(jnp.concatenate(cols, axis=1))
    s = jnp.concatenate(rows, axis=0)
    y_b = _dot((u * s).astype(BF16), wpb_ref[...])
    merged = ga_ref[...].astype(F32) * y_a + gb_ref[...].astype(F32) * y_b
    x1 = x_ref[...] + _dot(merged.astype(BF16), wout_ref[...])
    x1_ref[...] = x1
    h2 = _rms(x1, g2_ref[...])
    h2_hi = h2.astype(BF16)
    h2_ref[...] = h2_hi

    h2_lo = (h2 - h2_hi.astype(F32)).astype(BF16)
    logits = (_dot(h2_hi, wr_ref[0]) + (_dot(h2_lo, wr_ref[0]) + _dot(h2_hi, wr_ref[1]))
              + br_ref[...])
    lane = lax.broadcasted_iota(jnp.int32, logits.shape, 1)
    big = jnp.int32(4 * LANES)
    is_g = (lane >= N_EXPERTS) & (lane < N_EXPERTS + N_GROUPS)
    lg = jnp.where(is_g, logits, NEG)
    gmax = jnp.max(lg, axis=1, keepdims=True)
    g_w = 1.0 / jnp.sum(jnp.where(is_g, jnp.exp(lg - gmax), 0.0), axis=1, keepdims=True)
    g_idx = jnp.min(jnp.where(is_g & (lg == gmax), lane, big), axis=1, keepdims=True) - N_EXPERTS
    in_g = (lane < N_EXPERTS) & ((lane >> 3) == g_idx)
    le = jnp.where(in_g, logits, NEG)
    emax = jnp.max(le, axis=1, keepdims=True)
    ee = jnp.where(in_g, jnp.exp(le - emax), 0.0)
    ep = ee / jnp.sum(ee, axis=1, keepdims=True)
    ep1 = jnp.where(in_g, ep, -1.0)
    m1 = jnp.max(ep1, axis=1, keepdims=True)
    i1 = jnp.min(jnp.where(ep1 == m1, lane, big), axis=1, keepdims=True)
    ep2 = jnp.where(lane == i1, -1.0, ep1)
    m2 = jnp.max(ep2, axis=1, keepdims=True)
    i2 = jnp.min(jnp.where(ep2 == m2, lane, big), axis=1, keepdims=True)
    tot = m1 + m2
    comb_ref[...] = (jnp.where(lane == i1, g_w * m1 / tot, 0.0)
                     + jnp.where(lane == i2, g_w * m2 / tot, 0.0))
    goh_ref[...] = jnp.transpose(jnp.where(lane == g_idx, 1.0, 0.0))[0:SUBLANES, :]


def _mix(x2, olat_t, u, v, ga, gb, wuv, wpa, ws, bs, wpb, wout, g2, wr, br):
    n, d = x2.shape
    tm = TM_MIX
    row = lambda w: pl.BlockSpec((tm, w), lambda i: (i, 0))
    weights = (wuv, wpa, ws, bs, wpb, wout, g2, wr, br)
    return pl.pallas_call(
        _mix_kernel,
        grid=(n // tm,),
        in_specs=[row(d), pl.BlockSpec((KV_RANK, N_HEADS * tm), lambda i: (0, i)),
                  row(GMLP_WIDTH), row(GMLP_WIDTH), row(d), row(d)]
        + [_const_spec(w.shape) for w in weights],
        out_specs=(row(d), row(d), row(LANES), pl.BlockSpec((SUBLANES, tm), lambda i: (0, i))),
        out_shape=(jax.ShapeDtypeStruct((n, d), F32), jax.ShapeDtypeStruct((n, d), BF16),
                   jax.ShapeDtypeStruct((n, LANES), F32),
                   jax.ShapeDtypeStruct((SUBLANES, n), F32)),
        compiler_params=pltpu.CompilerParams(
            dimension_semantics=("arbitrary",), vmem_limit_bytes=VMEM_LIMIT),
        name="merge_router",
    )(x2, olat_t, u, v, ga, gb, *weights)


def _moe_kernel(x1_ref, h2_ref, comb_ref, goh_ref, tri_ref, wg_ref, wu_ref, wd_ref, gf_ref, o_ref,
                perm_ref, pos_ref, hs_ref, cs_ref, ys_ref, seg_ref):
    step = pl.program_id(1)
    tt, d = hs_ref.shape
    rb = MOE_ROWS
    pc = MOE_PERM_ROWS

    @pl.when(step == 0)
    def _():
        goh = goh_ref[...]
        before = _dot(goh.astype(BF16), tri_ref[...])
        pos = jnp.zeros((1, tt), F32)
        start = jnp.float32(0.0)
        for g in range(N_GROUPS):
            cnt = jnp.sum(goh[g:g + 1, :])
            seg_ref[g] = jnp.floor(start * (1.0 / rb)).astype(jnp.int32)
            seg_ref[N_GROUPS + g] = jnp.ceil((start + cnt) * (1.0 / rb)).astype(jnp.int32)
            pos = pos + goh[g:g + 1, :] * (start + before[g:g + 1, :])
            start = start + cnt
        pos_ref[...] = pos
        for r in range(tt // pc):
            dst = (lax.broadcasted_iota(jnp.int32, (pc, tt), 0) + r * pc).astype(F32)
            perm_ref[r * pc:(r + 1) * pc, :] = jnp.where(pos == dst, 1.0, 0.0).astype(BF16)
        perm = perm_ref[...]
        hs_ref[...] = _dot(perm, h2_ref[...]).astype(BF16)
        comb = comb_ref[...]
        comb_hi = comb.astype(BF16)
        comb_lo = (comb - comb_hi.astype(F32)).astype(BF16)
        cs_ref[...] = _dot(perm, comb_hi) + _dot(perm, comb_lo)
        ys_ref[...] = jnp.zeros(ys_ref.shape, F32)

    group = step // (EXPERTS_PER_GROUP // MOE_EXPERTS_PER_STEP)
    first = seg_ref[group]
    last = seg_ref[N_GROUPS + group]

    def block(b, carry):
        rows = pl.ds(pl.multiple_of(b * rb, rb), rb)
        hb = hs_ref[rows, :]
        cb = cs_ref[rows, :]
        lane = lax.broadcasted_iota(jnp.int32, cb.shape, 1)
        y = jnp.zeros((rb, d), F32)
        for k in range(MOE_EXPERTS_PER_STEP):
            e = step * MOE_EXPERTS_PER_STEP + k
            w_e = jnp.sum(jnp.where(lane == e, cb, 0.0), axis=1, keepdims=True)
            act = jax.nn.silu(_dot(hb, wg_ref[k])) * _dot(hb, wu_ref[k]) * w_e
            y = y + _dot(act.astype(BF16), wd_ref[k])
        ys_ref[rows, :] += y
        return carry

    lax.fori_loop(first, last, block, 0)

    @pl.when(step == pl.num_programs(1) - 1)
    def _():
        ys = ys_ref[...].astype(BF16)
        pos_col = jnp.transpose(jnp.broadcast_to(pos_ref[...], (SUBLANES, tt)))[:, 0:1]
        for r in range(tt // pc):
            rows = slice(r * pc, (r + 1) * pc)
            src = lax.broadcasted_iota(jnp.int32, (pc, tt), 1).astype(F32)
            unperm = jnp.where(pos_col[rows] == src, 1.0, 0.0).astype(BF16)
            o_ref[rows, :] = _rms(x1_ref[rows, :] + _dot(unperm, ys), gf_ref[...])


def _moe(x1, h2, comb, goh_t, wg, wu, wd, gf):
    n, d = x1.shape
    tt = min(TM_MOE, n)
    eps = MOE_EXPERTS_PER_STEP
    assert EXPERTS_PER_GROUP % eps == 0 and tt % MOE_ROWS == 0 and tt % MOE_PERM_ROWS == 0
    row = lambda w: pl.BlockSpec((tt, w), lambda i, s: (i, 0))
    tri = (jnp.arange(tt)[:, None] < jnp.arange(tt)[None, :]).astype(BF16)
    return pl.pallas_call(
        _moe_kernel,
        grid=(n // tt, N_EXPERTS // eps),
        in_specs=[row(d), row(d), row(LANES),
                  pl.BlockSpec((SUBLANES, tt), lambda i, s: (0, i)),
                  _const_spec(tri.shape),
                  pl.BlockSpec((eps, d, D_FF_EXPERT), lambda i, s: (s, 0, 0)),
                  pl.BlockSpec((eps, d, D_FF_EXPERT), lambda i, s: (s, 0, 0)),
                  pl.BlockSpec((eps, D_FF_EXPERT, d), lambda i, s: (s, 0, 0)),
                  _const_spec(gf.shape)],
        out_specs=row(d),
        out_shape=jax.ShapeDtypeStruct((n, d), F32),
        scratch_shapes=[
            pltpu.VMEM((tt, tt), BF16),
            pltpu.VMEM((1, tt), F32),
            pltpu.VMEM((tt, d), BF16),
            pltpu.VMEM((tt, LANES), F32),
            pltpu.VMEM((tt, d), F32),
            pltpu.SMEM((2 * N_GROUPS,), jnp.int32),
        ],
        compiler_params=pltpu.CompilerParams(
            dimension_semantics=("arbitrary", "arbitrary"), vmem_limit_bytes=VMEM_LIMIT),
        name="experts",
    )(x1, h2, comb, goh_t, tri, wg, wu, wd, gf)


def _t5_bucket(n):
    max_exact = NUM_BUCKETS // 2
    nf = jnp.maximum(n, 1).astype(F32)
    large = max_exact + jnp.floor(
        jnp.log(nf / max_exact) / np.float32(np.log(MAX_DISTANCE / max_exact))
        * (NUM_BUCKETS - max_exact)).astype(jnp.int32)
    large = jnp.minimum(large, NUM_BUCKETS - 1)
    return jnp.where(n < max_exact, n, large)


def _bias_tiles(rel_bias):
    qb = Q_BLK
    k = jnp.arange(qb, dtype=jnp.int32)[:, None]
    q = jnp.arange(qb, dtype=jnp.int32)[None, :]
    rel = (rel_bias - rel_bias[NUM_BUCKETS - 1]) * LOG2E

    def tile(dist):
        onehot = (_t5_bucket(dist)[:, :, None] == jnp.arange(NUM_BUCKETS)).astype(F32)
        t = jnp.einsum("kqb,bh->khq", onehot, rel, precision=lax.Precision.HIGHEST)
        return t.reshape(qb, N_HEADS * qb)

    return tile(jnp.maximum(q - k, 0)), tile(qb + q - k)


def _block_diag(w):
    h, a, b = w.shape
    eye = jnp.eye(h, dtype=w.dtype)
    return (eye[:, None, :, None] * w[:, :, None, :]).reshape(h * a, h * b)


def kernel(x, w_in, kv_norm_g, w_uk, w_uv, rel_bias, ln_v_g, ln_v_b, w_spatial, b_spatial,
           w_proj_a, w_proj_b, w_out, norm1_g, norm2_g, router_group_w, router_group_b,
           router_expert_w, router_expert_b, w_gate, w_up, w_down, final_norm_g):
    batch, seq, d = x.shape
    depth = w_in.shape[0]
    n = batch * seq
    assert depth == 1, "the final rms-norm is fused into the (single) layer's expert kernel"
    assert EXPERTS_PER_GROUP == 8
    assert seq % Q_BLK == 0 and n % min(TM_MOE, n) == 0
    assert TM_PROJ == Q_BLK == TM_MIX
    assert Q_BLK + 1 >= MAX_DISTANCE

    attn_w = N_HEADS * HEAD_DIM
    sizes = (attn_w, KV_RANK, IDX_HEADS * IDX_DIM, IDX_DIM, IDX_HEADS, GMLP_WIDTH, GMLP_WIDTH, d, d)
    offs = np.concatenate([[0], np.cumsum(sizes)])
    d0, d1 = _bias_tiles(rel_bias)
    tril = jnp.tril(jnp.ones((CHUNK, CHUNK), dtype=bool))
    row = lambda a: a.reshape(1, -1)

    x2 = x.reshape(n, d)
    l = 0
    wl = w_in[l].astype(BF16)
    wq, wc, wqi, wki, wwi, wu, wv, wga, wgb = [wl[:, offs[i]:offs[i + 1]] for i in range(9)]
    wt = jnp.concatenate([wq, wqi, wc, wwi], axis=1).T
    wuk_bd = _block_diag(jnp.transpose(w_uk[l], (1, 0, 2))).astype(BF16)
    wuv_t = jnp.transpose(w_uv[l], (1, 2, 0)).astype(BF16)

    qlat_t, qi_t, wi_t, ckv_t, ckv, ki, u, v, ga, gb = _projection(
        x2, row(norm1_g[l]), wt, wuk_bd, kv_norm_g[l].reshape(-1, 1), wc, row(kv_norm_g[l]),
        wki, wu, wv, row(ln_v_g[l]), row(ln_v_b[l]), wga, wgb)

    olat_t = _attention(qlat_t, qi_t, wi_t, ckv, ckv_t, ki, d0, d1, batch, seq)

    ws = jnp.where(tril[None], w_spatial[l], 0.0).astype(BF16)
    bs = jnp.broadcast_to(b_spatial[l][:, :, None], (GMLP_GROUPS, CHUNK, GMLP_GROUP_DIM))
    wr = jnp.concatenate(
        [router_expert_w[l], router_group_w[l],
         jnp.zeros((d, LANES - N_EXPERTS - N_GROUPS), F32)], axis=1)
    wr_hi = wr.astype(BF16)
    wr = jnp.stack([wr_hi, (wr - wr_hi.astype(F32)).astype(BF16)])
    br = jnp.concatenate(
        [router_expert_b[l], router_group_b[l],
         jnp.zeros((LANES - N_EXPERTS - N_GROUPS,), F32)]).reshape(1, LANES)
    x1, h2, comb, goh_t = _mix(
        x2, olat_t, u, v, ga, gb, wuv_t, w_proj_a[l].astype(BF16), ws, bs,
        w_proj_b[l].astype(BF16), w_out[l].astype(BF16), row(norm2_g[l]), wr, br)

    out = _moe(x1, h2, comb, goh_t, w_gate[l].astype(BF16), w_up[l].astype(BF16),
               w_down[l].astype(BF16), row(final_norm_g))
    return out.reshape(batch, seq, d)
```

```python
import functools

import numpy as np
import jax
import jax.numpy as jnp
from jax import lax
from jax.experimental import pallas as pl
from jax.experimental.pallas import tpu as pltpu

F32 = jnp.float32
BF16 = jnp.bfloat16

N_HEADS = 8
HEAD_DIM = 64
KV_RANK = 128
IDX_HEADS = 8
IDX_DIM = 64
TOPK_MAX = 256
CHUNK = 128
GMLP_GROUPS = 4
GMLP_GROUP_DIM = 128
GMLP_WIDTH = GMLP_GROUPS * GMLP_GROUP_DIM
NUM_BUCKETS = 32
MAX_DISTANCE = 128
N_GROUPS = 4
EXPERTS_PER_GROUP = 8
N_EXPERTS = N_GROUPS * EXPERTS_PER_GROUP
D_FF_EXPERT = 256
EPS = 1e-6

LANES = 128
SUBLANES = 8
VMEM_LIMIT = 52 * 1024 * 1024
NEG = -3.0e38
MASK = -1.0e30

TM_PROJ = 256
Q_BLK = 256
TM_MIX = 256
TM_MOE = 1024
MOE_ROWS = 128
MOE_PERM_ROWS = 256
MOE_EXPERTS_PER_STEP = 4
SEARCH_MAX_ITERS = 64
SEARCH_UNROLL = 2
COUNT_ACCS = 4
VERIFY_FROM = 20
VERIFY_EVERY = 8
LOG2E = 1.4426950408889634


def _dot(a, b):
    return jnp.dot(a, b, preferred_element_type=F32)


def _dot_t(a, b):
    return lax.dot_general(a, b, (((1,), (1,)), ((), ())), preferred_element_type=F32)


def _rms(x, g):
    return x * lax.rsqrt(jnp.mean(x * x, axis=-1, keepdims=True) + EPS) * g


_T_ROWS = (N_HEADS * HEAD_DIM, IDX_HEADS * IDX_DIM, KV_RANK, IDX_HEADS)
_T_OFFS = tuple(int(v) for v in np.concatenate([[0], np.cumsum(_T_ROWS)]))


def _proj_kernel(x_ref, g1_ref, wt_ref, wuk_ref, gkvc_ref, wc_ref, gkv_ref, wki_ref, wu_ref,
                 wv_ref, lng_ref, lnb_ref, wga_ref, wgb_ref,
                 qlat_ref, qi_ref, wi_ref, ckvt_ref, ckv_ref, ki_ref, u_ref, v_ref, ga_ref, gb_ref):
    hb = _rms(x_ref[...], g1_ref[...]).astype(BF16)
    t_all = _dot_t(wt_ref[...], hb)
    q_t, qi_t, c_t, w_t = [t_all[_T_OFFS[i]:_T_OFFS[i + 1]] for i in range(4)]
    qlat_t = (_dot(wuk_ref[...], q_t.astype(BF16)) * ((HEAD_DIM ** -0.5) * LOG2E)).astype(BF16)
    w_t = w_t * ((IDX_HEADS ** -0.5) * (IDX_DIM ** -0.5))
    tm = x_ref.shape[0]
    for h in range(N_HEADS):
        cols = slice(h * tm, (h + 1) * tm)
        qlat_ref[:, cols] = qlat_t[h * KV_RANK:(h + 1) * KV_RANK]
        qi_ref[:, cols] = qi_t[h * IDX_DIM:(h + 1) * IDX_DIM].astype(BF16)
        wi_ref[:, cols] = w_t[h:h + 1]
    c_n = c_t * lax.rsqrt(jnp.mean(c_t * c_t, axis=0, keepdims=True) + EPS) * gkvc_ref[...]
    ckvt_ref[...] = c_n.astype(BF16)
    ckv_ref[...] = _rms(_dot(hb, wc_ref[...]), gkv_ref[...]).astype(BF16)
    ki_ref[...] = _dot(hb, wki_ref[...]).astype(BF16)
    u_ref[...] = jax.nn.gelu(_dot(hb, wu_ref[...])).astype(BF16)
    v = jax.nn.gelu(_dot(hb, wv_ref[...]))
    mu = jnp.mean(v, axis=-1, keepdims=True)
    var = jnp.mean(jnp.square(v - mu), axis=-1, keepdims=True)
    v_ref[...] = ((v - mu) * lax.rsqrt(var + EPS) * lng_ref[...] + lnb_ref[...]).astype(BF16)
    ga_ref[...] = jax.nn.sigmoid(_dot(hb, wga_ref[...])).astype(BF16)
    gb_ref[...] = jax.nn.sigmoid(_dot(hb, wgb_ref[...])).astype(BF16)


def _const_spec(shape):
    nd = len(shape)
    return pl.BlockSpec(shape, lambda *_: (0,) * nd)


def _projection(x2, g1, wt, wuk, gkvc, wc, gkv, wki, wu, wv, lng, lnb, wga, wgb):
    n, d = x2.shape
    tm = TM_PROJ
    row = lambda w: pl.BlockSpec((tm, w), lambda i: (i, 0))
    col = lambda h: pl.BlockSpec((h, tm), lambda i: (0, i))
    wide = lambda h: pl.BlockSpec((h, N_HEADS * tm), lambda i: (0, i))
    weights = (g1, wt, wuk, gkvc, wc, gkv, wki, wu, wv, lng, lnb, wga, wgb)
    out_shape = (
        jax.ShapeDtypeStruct((KV_RANK, N_HEADS * n), BF16),
        jax.ShapeDtypeStruct((IDX_DIM, IDX_HEADS * n), BF16),
        jax.ShapeDtypeStruct((1, IDX_HEADS * n), F32),
        jax.ShapeDtypeStruct((KV_RANK, n), BF16),
        jax.ShapeDtypeStruct((n, KV_RANK), BF16),
        jax.ShapeDtypeStruct((n, IDX_DIM), BF16),
        jax.ShapeDtypeStruct((n, GMLP_WIDTH), BF16),
        jax.ShapeDtypeStruct((n, GMLP_WIDTH), BF16),
        jax.ShapeDtypeStruct((n, d), BF16),
        jax.ShapeDtypeStruct((n, d), BF16),
    )
    out_specs = (
        wide(KV_RANK), wide(IDX_DIM), wide(1), col(KV_RANK),
        row(KV_RANK), row(IDX_DIM), row(GMLP_WIDTH), row(GMLP_WIDTH), row(d), row(d),
    )
    return pl.pallas_call(
        _proj_kernel,
        grid=(n // tm,),
        in_specs=[row(d)] + [_const_spec(w.shape) for w in weights],
        out_specs=out_specs,
        out_shape=out_shape,
        compiler_params=pltpu.CompilerParams(
            dimension_semantics=("arbitrary",), vmem_limit_bytes=VMEM_LIMIT),
        name="in_projection",
    )(x2, *weights)


def _attn_kernel(qlat_ref, qi_ref, wi_ref, ckv_ref, ckvt_ref, ki_ref, d0_ref, d1_ref, o_ref,
                 idx_ref, m_ref, l_ref, acc_ref, *, topk, seq):
    qb = Q_BLK
    ns = qb // SUBLANES
    j_q = pl.program_id(1)
    q0 = j_q * qb
    nch = j_q + 1
    kf = float(topk)

    def chunk_rows(j):
        return pl.ds(pl.multiple_of(j * qb, qb), qb)

    krow = lax.broadcasted_iota(jnp.int32, (qb, qb), 0)
    qcol = lax.broadcasted_iota(jnp.int32, (qb, qb), 1)

    def idx_chunk(j, carry):
        rmax, rmin = carry
        zw = jnp.maximum(_dot(ki_ref[chunk_rows(j), :], qi_ref[...]), 0.0) * wi_ref[...]
        index = zw[:, 0:qb]
        for h in range(1, IDX_HEADS):
            index = index + zw[:, h * qb:(h + 1) * qb]
        valid = (j * qb + krow) <= (q0 + qcol)
        x = jnp.where(valid, index, NEG)
        idx_ref[chunk_rows(j), :] = x
        rmax = jnp.maximum(rmax, jnp.max(x, axis=0, keepdims=True))
        rmin = jnp.minimum(rmin, jnp.min(jnp.where(valid, index, -NEG), axis=0, keepdims=True))
        return rmax, rmin

    rmax, rmin = lax.fori_loop(
        0, nch, idx_chunk, (jnp.full((1, qb), NEG, F32), jnp.full((1, qb), -NEG, F32)))

    @pl.when((nch & 1) == 1)
    def _():
        idx_ref[chunk_rows(nch), :] = jnp.full((qb, qb), NEG, F32)

    def slabs(j):
        return idx_ref[chunk_rows(j), :].reshape(ns, SUBLANES, qb)

    def col_sum(c8):
        return jnp.sum(c8, axis=0, keepdims=True)

    def count_ge(thr):
        thr8 = jnp.broadcast_to(thr, (SUBLANES, qb))

        def body(jp, accs):
            pair = idx_ref[pl.ds(pl.multiple_of(jp * (2 * qb), 2 * qb), 2 * qb), :]
            accs = list(accs)
            for s in range(2 * ns):
                xs = pair[s * SUBLANES:(s + 1) * SUBLANES]
                accs[s % COUNT_ACCS] = accs[s % COUNT_ACCS] + jnp.where(xs >= thr8, 1.0, 0.0)
            return tuple(accs)

        zero = jnp.zeros((SUBLANES, qb), F32)
        accs = lax.fori_loop(0, (nch + 1) >> 1, body, (zero,) * COUNT_ACCS)
        return col_sum(functools.reduce(lambda a, b: a + b, accs))

    def bracket_is_tie(lo, hi):
        lo8 = jnp.broadcast_to(lo, (SUBLANES, qb))[None]
        hi8 = jnp.broadcast_to(hi, (SUBLANES, qb))[None]

        def body(j, carry):
            bmax, bmin = carry
            xs = slabs(j)
            inb = (xs >= lo8) & (xs < hi8)
            bmax = jnp.maximum(bmax, jnp.max(jnp.where(inb, xs, NEG), axis=0))
            bmin = jnp.minimum(bmin, jnp.min(jnp.where(inb, xs, -NEG), axis=0))
            return bmax, bmin

        bmax, bmin = lax.fori_loop(
            0, nch, body,
            (jnp.full((SUBLANES, qb), NEG, F32), jnp.full((SUBLANES, qb), -NEG, F32)))
        bmax = jnp.max(bmax, axis=0, keepdims=True)
        bmin = jnp.min(bmin, axis=0, keepdims=True)
        return jnp.where(bmax == bmin, 1.0, 0.0)

    nvalid = (q0 + 1 + lax.broadcasted_iota(jnp.int32, (1, qb), 1)).astype(F32)
    hi0 = rmax + (jnp.abs(rmax) * 1e-6 + 1e-30)

    def open_rows(c_lo, tie):
        return jnp.max(jnp.where((c_lo > kf) & (tie < 0.5), 1.0, 0.0)) > 0.0

    def search_cond(state):
        it, _, _, c_lo, _, tie = state
        return (it < SEARCH_MAX_ITERS) & open_rows(c_lo, tie)

    def search_body(state):
        it, lo, hi, c_lo, c_hi, tie = state
        for _ in range(SEARCH_UNROLL):
            mid = 0.5 * (lo + hi)
            c = count_ge(mid)
            ok = c >= kf
            lo, hi = jnp.where(ok, mid, lo), jnp.where(ok, hi, mid)
            c_lo, c_hi = jnp.where(ok, c, c_lo), jnp.where(ok, c_hi, c)
        it = it + SEARCH_UNROLL
        check = (it >= VERIFY_FROM) & (((it - VERIFY_FROM) & (VERIFY_EVERY - 1)) == 0)
        tie = lax.cond(check, lambda: bracket_is_tie(lo, hi), lambda: tie)
        return it, lo, hi, c_lo, c_hi, tie

    _, lo, hi, c_lo, c_hi, _ = lax.while_loop(
        search_cond, search_body,
        (jnp.int32(0), rmin, hi0, nvalid, jnp.zeros((1, qb), F32), jnp.zeros((1, qb), F32)))

    excess = c_lo > kf

    @pl.when(jnp.max(jnp.where(excess, 1.0, 0.0)) > 0.0)
    def _():
        need = jnp.where(excess, kf - c_hi, float(seq + 1))
        tri = jnp.where(lax.broadcasted_iota(jnp.int32, (qb, qb), 1)
                        <= lax.broadcasted_iota(jnp.int32, (qb, qb), 0), 1.0, 0.0).astype(BF16)

        def demote(j, seen):
            xs = idx_ref[chunk_rows(j), :]
            inb = (xs >= lo) & (xs < hi)
            inb_f = jnp.where(inb, 1.0, 0.0)
            rank = seen + _dot(tri, inb_f.astype(BF16))
            idx_ref[chunk_rows(j), :] = jnp.where(inb & (rank > need), NEG, xs)
            return seen + jnp.sum(inb_f, axis=0, keepdims=True)

        lax.fori_loop(0, nch, demote, jnp.zeros((1, qb), F32))

    m_ref[...] = jnp.full(m_ref.shape, MASK, F32)
    l_ref[...] = jnp.zeros(l_ref.shape, F32)
    acc_ref[...] = jnp.zeros(acc_ref.shape, F32)

    def att_step(j, bias_ref):
        ckc = ckv_ref[chunk_rows(j), :]
        ckc_t = ckvt_ref[:, chunk_rows(j)]
        madd = jnp.where(idx_ref[chunk_rows(j), :] >= lo, 0.0, MASK)
        s_all = _dot(ckc, qlat_ref[...])
        s_h = []
        for h in range(N_HEADS):
            s = s_all[:, h * qb:(h + 1) * qb] + madd
            if bias_ref is not None:
                s = s + bias_ref[:, h * qb:(h + 1) * qb]
            s_h.append(s)
        s_all = jnp.concatenate(s_h, axis=1)
        m_old = m_ref[...]
        m_new = jnp.maximum(m_old, jnp.max(s_all, axis=0, keepdims=True))
        alpha = jnp.exp2(m_old - m_new)
        p = jnp.exp2(s_all - m_new)
        l_ref[...] = alpha * l_ref[...] + jnp.sum(p, axis=0, keepdims=True)
        acc_ref[...] = alpha * acc_ref[...] + _dot(ckc_t, p.astype(BF16))
        m_ref[...] = m_new

    def far_step(j, carry):
        att_step(j, None)
        return carry

    lax.fori_loop(0, jnp.maximum(nch - 2, 0), far_step, 0)

    @pl.when(nch >= 2)
    def _():
        att_step(nch - 2, d1_ref)

    att_step(nch - 1, d0_ref)

    o_ref[...] = (acc_ref[...] / l_ref[...]).astype(BF16)


def _attention(qlat_w, qi_w, wi_w, ckv, ckv_t, ki, d0, d1, batch, seq):
    qb = Q_BLK
    nq = seq // qb
    n = batch * seq
    topk = min(TOPK_MAX, seq // 4)
    kern = functools.partial(_attn_kernel, topk=topk, seq=seq)
    wide = lambda h: pl.BlockSpec((h, N_HEADS * qb), lambda b, j: (0, b * nq + j))
    return pl.pallas_call(
        kern,
        grid=(batch, nq),
        in_specs=[
            wide(KV_RANK), wide(IDX_DIM), wide(1),
            pl.BlockSpec((seq, KV_RANK), lambda b, j: (b, 0)),
            pl.BlockSpec((KV_RANK, seq), lambda b, j: (0, b)),
            pl.BlockSpec((seq, IDX_DIM), lambda b, j: (b, 0)),
            _const_spec(d0.shape), _const_spec(d1.shape),
        ],
        out_specs=wide(KV_RANK),
        out_shape=jax.ShapeDtypeStruct((KV_RANK, N_HEADS * n), BF16),
        scratch_shapes=[
            pltpu.VMEM((seq + qb, qb), F32),
            pltpu.VMEM((1, N_HEADS * qb), F32),
            pltpu.VMEM((1, N_HEADS * qb), F32),
            pltpu.VMEM((KV_RANK, N_HEADS * qb), F32),
        ],
        compiler_params=pltpu.CompilerParams(
            dimension_semantics=("arbitrary", "arbitrary"), vmem_limit_bytes=VMEM_LIMIT),
        name="sparse_attention",
    )(qlat_w, qi_w, wi_w, ckv, ckv_t, ki, d0, d1)


def _mix_kernel(x_ref, ol_ref, u_ref, v_ref, ga_ref, gb_ref, wuv_ref, wpa_ref, ws_ref, bs_ref,
                wpb_ref, wout_ref, g2_ref, wr_ref, br_ref, x1_ref, h2_ref, comb_ref, goh_ref):
    tm = x_ref.shape[0]
    o_at = jnp.concatenate(
        [_dot(wuv_ref[h], ol_ref[:, h * tm:(h + 1) * tm]) for h in range(N_HEADS)], axis=0)
    y_a = _dot(jnp.transpose(o_at).astype(BF16), wpa_ref[...])
    v = v_ref[...]
    u = u_ref[...].astype(F32)
    rows = []
    for c in range(tm // CHUNK):
        cols = []
        for g in range(GMLP_GROUPS):
            vc = v[c * CHUNK:(c + 1) * CHUNK, g * GMLP_GROUP_DIM:(g + 1) * GMLP_GROUP_DIM]
            cols.append(_dot(ws_ref[g], vc) + bs_ref[g])
        rows.append(jnp.concatenate(cols, axis=1))
    s = jnp.concatenate(rows, axis=0)
    y_b = _dot((u * s).astype(BF16), wpb_ref[...])
    merged = ga_ref[...].astype(F32) * y_a + gb_ref[...].astype(F32) * y_b
    x1 = x_ref[...] + _dot(merged.astype(BF16), wout_ref[...])
    x1_ref[...] = x1
    h2 = _rms(x1, g2_ref[...])
    h2_hi = h2.astype(BF16)
    h2_ref[...] = h2_hi

    h2_lo = (h2 - h2_hi.astype(F32)).astype(BF16)
    logits = (_dot(h2_hi, wr_ref[0]) + (_dot(h2_lo, wr_ref[0]) + _dot(h2_hi, wr_ref[1]))
              + br_ref[...])
    lane = lax.broadcasted_iota(jnp.int32, logits.shape, 1)
    big = jnp.int32(4 * LANES)
    is_g = (lane >= N_EXPERTS) & (lane < N_EXPERTS + N_GROUPS)
    lg = jnp.where(is_g, logits, NEG)
    gmax = jnp.max(lg, axis=1, keepdims=True)
    g_w = 1.0 / jnp.sum(jnp.where(is_g, jnp.exp(lg - gmax), 0.0), axis=1, keepdims=True)
    g_idx = jnp.min(jnp.where(is_g & (lg == gmax), lane, big), axis=1, keepdims=True) - N_EXPERTS
    in_g = (lane < N_EXPERTS) & ((lane >> 3) == g_idx)
    le = jnp.where(in_g, logits, NEG)
    emax = jnp.max(le, axis=1, keepdims=True)
    ee = jnp.where(in_g, jnp.exp(le - emax), 0.0)
    ep = ee / jnp.sum(ee, axis=1, keepdims=True)
    ep1 = jnp.where(in_g, ep, -1.0)
    m1 = jnp.max(ep1, axis=1, keepdims=True)
    i1 = jnp.min(jnp.where(ep1 == m1, lane, big), axis=1, keepdims=True)
    ep2 = jnp.where(lane == i1, -1.0, ep1)
    m2 = jnp.max(ep2, axis=1, keepdims=True)
    i2 = jnp.min(jnp.where(ep2 == m2, lane, big), axis=1, keepdims=True)
    tot = m1 + m2
    comb_ref[...] = (jnp.where(lane == i1, g_w * m1 / tot, 0.0)
                     + jnp.where(lane == i2, g_w * m2 / tot, 0.0))
    goh_ref[...] = jnp.transpose(jnp.where(lane == g_idx, 1.0, 0.0))[0:SUBLANES, :]


def _mix(x2, olat_t, u, v, ga, gb, wuv, wpa, ws, bs, wpb, wout, g2, wr, br):
    n, d = x2.shape
    tm = TM_MIX
    row = lambda w: pl.BlockSpec((tm, w), lambda i: (i, 0))
    weights = (wuv, wpa, ws, bs, wpb, wout, g2, wr, br)
    return pl.pallas_call(
        _mix_kernel,
        grid=(n // tm,),
        in_specs=[row(d), pl.BlockSpec((KV_RANK, N_HEADS * tm), lambda i: (0, i)),
                  row(GMLP_WIDTH), row(GMLP_WIDTH), row(d), row(d)]
        + [_const_spec(w.shape) for w in weights],
        out_specs=(row(d), row(d), row(LANES), pl.BlockSpec((SUBLANES, tm), lambda i: (0, i))),
        out_shape=(jax.ShapeDtypeStruct((n, d), F32), jax.ShapeDtypeStruct((n, d), BF16),
                   jax.ShapeDtypeStruct((n, LANES), F32),
                   jax.ShapeDtypeStruct((SUBLANES, n), F32)),
        compiler_params=pltpu.CompilerParams(
            dimension_semantics=("arbitrary",), vmem_limit_bytes=VMEM_LIMIT),
        name="merge_router",
    )(x2, olat_t, u, v, ga, gb, *weights)


def _moe_kernel(x1_ref, h2_ref, comb_ref, goh_ref, tri_ref, wg_ref, wu_ref, wd_ref, gf_ref, o_ref,
                perm_ref, pos_ref, hs_ref, cs_ref, ys_ref, seg_ref):
    step = pl.program_id(1)
    tt, d = hs_ref.shape
    rb = MOE_ROWS
    pc = MOE_PERM_ROWS

    @pl.when(step == 0)
    def _():
        goh = goh_ref[...]
        before = _dot(goh.astype(BF16), tri_ref[...])
        pos = jnp.zeros((1, tt), F32)
        start = jnp.float32(0.0)
        for g in range(N_GROUPS):
            cnt = jnp.sum(goh[g:g + 1, :])
            seg_ref[g] = jnp.floor(start * (1.0 / rb)).astype(jnp.int32)
            seg_ref[N_GROUPS + g] = jnp.ceil((start + cnt) * (1.0 / rb)).astype(jnp.int32)
            pos = pos + goh[g:g + 1, :] * (start + before[g:g + 1, :])
            start = start + cnt
        pos_ref[...] = pos
        for r in range(tt // pc):
            dst = (lax.broadcasted_iota(jnp.int32, (pc, tt), 0) + r * pc).astype(F32)
            perm_ref[r * pc:(r + 1) * pc, :] = jnp.where(pos == dst, 1.0, 0.0).astype(BF16)
        perm = perm_ref[...]
        hs_ref[...] = _dot(perm, h2_ref[...]).astype(BF16)
        comb = comb_ref[...]
        comb_hi = comb.astype(BF16)
        comb_lo = (comb - comb_hi.astype(F32)).astype(BF16)
        cs_ref[...] = _dot(perm, comb_hi) + _dot(perm, comb_lo)
        ys_ref[...] = jnp.zeros(ys_ref.shape, F32)

    group = step // (EXPERTS_PER_GROUP // MOE_EXPERTS_PER_STEP)
    first = seg_ref[group]
    last = seg_ref[N_GROUPS + group]

    def block(b, carry):
        rows = pl.ds(pl.multiple_of(b * rb, rb), rb)
        hb = hs_ref[rows, :]
        cb = cs_ref[rows, :]
        lane = lax.broadcasted_iota(jnp.int32, cb.shape, 1)
        y = jnp.zeros((rb, d), F32)
        for k in range(MOE_EXPERTS_PER_STEP):
            e = step * MOE_EXPERTS_PER_STEP + k
            w_e = jnp.sum(jnp.where(lane == e, cb, 0.0), axis=1, keepdims=True)
            act = jax.nn.silu(_dot(hb, wg_ref[k])) * _dot(hb, wu_ref[k]) * w_e
            y = y + _dot(act.astype(BF16), wd_ref[k])
        ys_ref[rows, :] += y
        return carry

    lax.fori_loop(first, last, block, 0)

    @pl.when(step == pl.num_programs(1) - 1)
    def _():
        ys = ys_ref[...].astype(BF16)
        pos_col = jnp.transpose(jnp.broadcast_to(pos_ref[...], (SUBLANES, tt)))[:, 0:1]
        for r in range(tt // pc):
            rows = slice(r * pc, (r + 1) * pc)
            src = lax.broadcasted_iota(jnp.int32, (pc, tt), 1).astype(F32)
            unperm = jnp.where(pos_col[rows] == src, 1.0, 0.0).astype(BF16)
            o_ref[rows, :] = _rms(x1_ref[rows, :] + _dot(unperm, ys), gf_ref[...])


def _moe(x1, h2, comb, goh_t, wg, wu, wd, gf):
    n, d = x1.shape
    tt = min(TM_MOE, n)
    eps = MOE_EXPERTS_PER_STEP
    assert EXPERTS_PER_GROUP % eps == 0 and tt % MOE_ROWS == 0 and tt % MOE_PERM_ROWS == 0
    row = lambda w: pl.BlockSpec((tt, w), lambda i, s: (i, 0))
    tri = (jnp.arange(tt)[:, None] < jnp.arange(tt)[None, :]).astype(BF16)
    return pl.pallas_call(
        _moe_kernel,
        grid=(n // tt, N_EXPERTS // eps),
        in_specs=[row(d), row(d), row(LANES),
                  pl.BlockSpec((SUBLANES, tt), lambda i, s: (0, i)),
                  _const_spec(tri.shape),
                  pl.BlockSpec((eps, d, D_FF_EXPERT), lambda i, s: (s, 0, 0)),
                  pl.BlockSpec((eps, d, D_FF_EXPERT), lambda i, s: (s, 0, 0)),
                  pl.BlockSpec((eps, D_FF_EXPERT, d), lambda i, s: (s, 0, 0)),
                  _const_spec(gf.shape)],
        out_specs=row(d),
        out_shape=jax.ShapeDtypeStruct((n, d), F32),
        scratch_shapes=[
            pltpu.VMEM((tt, tt), BF16),
            pltpu.VMEM((1, tt), F32),
            pltpu.VMEM((tt, d), BF16),
            pltpu.VMEM((tt, LANES), F32),
            pltpu.VMEM((tt, d), F32),
            pltpu.SMEM((2 * N_GROUPS,), jnp.int32),
        ],
        compiler_params=pltpu.CompilerParams(
            dimension_semantics=("arbitrary", "arbitrary"), vmem_limit_bytes=VMEM_LIMIT),
        name="experts",
    )(x1, h2, comb, goh_t, tri, wg, wu, wd, gf)


def _t5_bucket(n):
    max_exact = NUM_BUCKETS // 2
    nf = jnp.maximum(n, 1).astype(F32)
    large = max_exact + jnp.floor(
        jnp.log(nf / max_exact) / np.float32(np.log(MAX_DISTANCE / max_exact))
        * (NUM_BUCKETS - max_exact)).astype(jnp.int32)
    large = jnp.minimum(large, NUM_BUCKETS - 1)
    return jnp.where(n < max_exact, n, large)


def _bias_tiles(rel_bias):
    qb = Q_BLK
    k = jnp.arange(qb, dtype=jnp.int32)[:, None]
    q = jnp.arange(qb, dtype=jnp.int32)[None, :]
    rel = (rel_bias - rel_bias[NUM_BUCKETS - 1]) * LOG2E

    def tile(dist):
        onehot = (_t5_bucket(dist)[:, :, None] == jnp.arange(NUM_BUCKETS)).astype(F32)
        t = jnp.einsum("kqb,bh->khq", onehot, rel, precision=lax.Precision.HIGHEST)
        return t.reshape(qb, N_HEADS * qb)

    return tile(jnp.maximum(q - k, 0)), tile(qb + q - k)


def _block_diag(w):
    h, a, b = w.shape
    eye = jnp.eye(h, dtype=w.dtype)
    return (eye[:, None, :, None] * w[:, :, None, :]).reshape(h * a, h * b)


def kernel(x, w_in, kv_norm_g, w_uk, w_uv, rel_bias, ln_v_g, ln_v_b, w_spatial, b_spatial,
           w_proj_a, w_proj_b, w_out, norm1_g, norm2_g, router_group_w, router_group_b,
           router_expert_w, router_expert_b, w_gate, w_up, w_down, final_norm_g):
    batch, seq, d = x.shape
    depth = w_in.shape[0]
    n = batch * seq
    assert depth == 1, "the final rms-norm is fused into the (single) layer's expert kernel"
    assert EXPERTS_PER_GROUP == 8
    assert seq % Q_BLK == 0 and n % min(TM_MOE, n) == 0
    assert TM_PROJ == Q_BLK == TM_MIX
    assert Q_BLK + 1 >= MAX_DISTANCE

    attn_w = N_HEADS * HEAD_DIM
    sizes = (attn_w, KV_RANK, IDX_HEADS * IDX_DIM, IDX_DIM, IDX_HEADS, GMLP_WIDTH, GMLP_WIDTH, d, d)
    offs = np.concatenate([[0], np.cumsum(sizes)])
    d0, d1 = _bias_tiles(rel_bias)
    tril = jnp.tril(jnp.ones((CHUNK, CHUNK), dtype=bool))
    row = lambda a: a.reshape(1, -1)

    x2 = x.reshape(n, d)
    l = 0
    wl = w_in[l].astype(BF16)
    wq, wc, wqi, wki, wwi, wu, wv, wga, wgb = [wl[:, offs[i]:offs[i + 1]] for i in range(9)]
    wt = jnp.concatenate([wq, wqi, wc, wwi], axis=1).T
    wuk_bd = _block_diag(jnp.transpose(w_uk[l], (1, 0, 2))).astype(BF16)
    wuv_t = jnp.transpose(w_uv[l], (1, 2, 0)).astype(BF16)

    qlat_t, qi_t, wi_t, ckv_t, ckv, ki, u, v, ga, gb = _projection(
        x2, row(norm1_g[l]), wt, wuk_bd, kv_norm_g[l].reshape(-1, 1), wc, row(kv_norm_g[l]),
        wki, wu, wv, row(ln_v_g[l]), row(ln_v_b[l]), wga, wgb)

    olat_t = _attention(qlat_t, qi_t, wi_t, ckv, ckv_t, ki, d0, d1, batch, seq)

    ws = jnp.where(tril[None], w_spatial[l], 0.0).astype(BF16)
    bs = jnp.broadcast_to(b_spatial[l][:, :, None], (GMLP_GROUPS, CHUNK, GMLP_GROUP_DIM))
    wr = jnp.concatenate(
        [router_expert_w[l], router_group_w[l],
         jnp.zeros((d, LANES - N_EXPERTS - N_GROUPS), F32)], axis=1)
    wr_hi = wr.astype(BF16)
    wr = jnp.stack([wr_hi, (wr - wr_hi.astype(F32)).astype(BF16)])
    br = jnp.concatenate(
        [router_expert_b[l], router_group_b[l],
         jnp.zeros((LANES - N_EXPERTS - N_GROUPS,), F32)]).reshape(1, LANES)
    x1, h2, comb, goh_t = _mix(
        x2, olat_t, u, v, ga, gb, wuv_t, w_proj_a[l].astype(BF16), ws, bs,
        w_proj_b[l].astype(BF16), w_out[l].astype(BF16), row(norm2_g[l]), wr, br)

    out = _moe(x1, h2, comb, goh_t, w_gate[l].astype(BF16), w_up[l].astype(BF16),
               w_down[l].astype(BF16), row(final_norm_g))
    return out.reshape(batch, seq, d)
```

```python
import functools

import numpy as np
import jax
import jax.numpy as jnp
from jax import lax
from jax.experimental import pallas as pl
from jax.experimental.pallas import tpu as pltpu

F32 = jnp.float32
BF16 = jnp.bfloat16

N_HEADS = 8
HEAD_DIM = 64
KV_RANK = 128
IDX_HEADS = 8
IDX_DIM = 64
TOPK_MAX = 256
CHUNK = 128
GMLP_GROUPS = 4
GMLP_GROUP_DIM = 128
GMLP_WIDTH = GMLP_GROUPS * GMLP_GROUP_DIM
NUM_BUCKETS = 32
MAX_DISTANCE = 128
N_GROUPS = 4
EXPERTS_PER_GROUP = 8
N_EXPERTS = N_GROUPS * EXPERTS_PER_GROUP
D_FF_EXPERT = 256
EPS = 1e-6

LANES = 128
SUBLANES = 8
VMEM_LIMIT = 52 * 1024 * 1024
NEG = -3.0e38
MASK = -1.0e30

TM_PROJ = 256
Q_BLK = 256
TM_MIX = 256
TM_MOE = 1024
MOE_ROWS = 128
MOE_PERM_ROWS = 256
MOE_EXPERTS_PER_STEP = 4
SEARCH_MAX_ITERS = 64
SEARCH_UNROLL = 2
COUNT_ACCS = 4
VERIFY_FROM = 16
VERIFY_EVERY = 4
LOG2E = 1.4426950408889634


def _dot(a, b):
    return jnp.dot(a, b, preferred_element_type=F32)


def _dot_t(a, b):
    return lax.dot_general(a, b, (((1,), (1,)), ((), ())), preferred_element_type=F32)


def _rms(x, g):
    return x * lax.rsqrt(jnp.mean(x * x, axis=-1, keepdims=True) + EPS) * g


_T_ROWS = (N_HEADS * HEAD_DIM, IDX_HEADS * IDX_DIM, KV_RANK, IDX_HEADS)
_T_OFFS = tuple(int(v) for v in np.concatenate([[0], np.cumsum(_T_ROWS)]))


def _proj_kernel(x_ref, g1_ref, wt_ref, wuk_ref, gkvc_ref, wc_ref, gkv_ref, wki_ref, wu_ref,
                 wv_ref, lng_ref, lnb_ref, wga_ref, wgb_ref,
                 qlat_ref, qi_ref, wi_ref, ckvt_ref, ckv_ref, ki_ref, u_ref, v_ref, ga_ref, gb_ref):
    hb = _rms(x_ref[...], g1_ref[...]).astype(BF16)
    t_all = _dot_t(wt_ref[...], hb)
    q_t, qi_t, c_t, w_t = [t_all[_T_OFFS[i]:_T_OFFS[i + 1]] for i in range(4)]
    qlat_t = (_dot(wuk_ref[...], q_t.astype(BF16)) * ((HEAD_DIM ** -0.5) * LOG2E)).astype(BF16)
    w_t = w_t * ((IDX_HEADS ** -0.5) * (IDX_DIM ** -0.5))
    tm = x_ref.shape[0]
    for h in range(N_HEADS):
        cols = slice(h * tm, (h + 1) * tm)
        qlat_ref[:, cols] = qlat_t[h * KV_RANK:(h + 1) * KV_RANK]
        qi_ref[:, cols] = qi_t[h * IDX_DIM:(h + 1) * IDX_DIM].astype(BF16)
        wi_ref[:, cols] = w_t[h:h + 1]
    c_n = c_t * lax.rsqrt(jnp.mean(c_t * c_t, axis=0, keepdims=True) + EPS) * gkvc_ref[...]
    ckvt_ref[...] = c_n.astype(BF16)
    ckv_ref[...] = _rms(_dot(hb, wc_ref[...]), gkv_ref[...]).astype(BF16)
    ki_ref[...] = _dot(hb, wki_ref[...]).astype(BF16)
    u_ref[...] = jax.nn.gelu(_dot(hb, wu_ref[...])).astype(BF16)
    v = jax.nn.gelu(_dot(hb, wv_ref[...]))
    mu = jnp.mean(v, axis=-1, keepdims=True)
    var = jnp.mean(jnp.square(v - mu), axis=-1, keepdims=True)
    v_ref[...] = ((v - mu) * lax.rsqrt(var + EPS) * lng_ref[...] + lnb_ref[...]).astype(BF16)
    ga_ref[...] = jax.nn.sigmoid(_dot(hb, wga_ref[...])).astype(BF16)
    gb_ref[...] = jax.nn.sigmoid(_dot(hb, wgb_ref[...])).astype(BF16)


def _const_spec(shape):
    nd = len(shape)
    return pl.BlockSpec(shape, lambda *_: (0,) * nd)


def _projection(x2, g1, wt, wuk, gkvc, wc, gkv, wki, wu, wv, lng, lnb, wga, wgb):
    n, d = x2.shape
    tm = TM_PROJ
    row = lambda w: pl.BlockSpec((tm, w), lambda i: (i, 0))
    col = lambda h: pl.BlockSpec((h, tm), lambda i: (0, i))
    wide = lambda h: pl.BlockSpec((h, N_HEADS * tm), lambda i: (0, i))
    weights = (g1, wt, wuk, gkvc, wc, gkv, wki, wu, wv, lng, lnb, wga, wgb)
    out_shape = (
        jax.ShapeDtypeStruct((KV_RANK, N_HEADS * n), BF16),
        jax.ShapeDtypeStruct((IDX_DIM, IDX_HEADS * n), BF16),
        jax.ShapeDtypeStruct((1, IDX_HEADS * n), F32),
        jax.ShapeDtypeStruct((KV_RANK, n), BF16),
        jax.ShapeDtypeStruct((n, KV_RANK), BF16),
        jax.ShapeDtypeStruct((n, IDX_DIM), BF16),
        jax.ShapeDtypeStruct((n, GMLP_WIDTH), BF16),
        jax.ShapeDtypeStruct((n, GMLP_WIDTH), BF16),
        jax.ShapeDtypeStruct((n, d), BF16),
        jax.ShapeDtypeStruct((n, d), BF16),
    )
    out_specs = (
        wide(KV_RANK), wide(IDX_DIM), wide(1), col(KV_RANK),
        row(KV_RANK), row(IDX_DIM), row(GMLP_WIDTH), row(GMLP_WIDTH), row(d), row(d),
    )
    return pl.pallas_call(
        _proj_kernel,
        grid=(n // tm,),
        in_specs=[row(d)] + [_const_spec(w.shape) for w in weights],
        out_specs=out_specs,
        out_shape=out_shape,
        compiler_params=pltpu.CompilerParams(
            dimension_semantics=("arbitrary",), vmem_limit_bytes=VMEM_LIMIT),
        name="in_projection",
    )(x2, *weights)


def _attn_kernel(qlat_ref, qi_ref, wi_ref, ckv_ref, ckvt_ref, ki_ref, d0_ref, d1_ref, o_ref,
                 idx_ref, m_ref, l_ref, acc_ref, *, topk, seq):
    qb = Q_BLK
    ns = qb // SUBLANES
    j_q = pl.program_id(1)
    q0 = j_q * qb
    nch = j_q + 1
    kf = float(topk)

    def chunk_rows(j):
        return pl.ds(pl.multiple_of(j * qb, qb), qb)

    krow = lax.broadcasted_iota(jnp.int32, (qb, qb), 0)
    qcol = lax.broadcasted_iota(jnp.int32, (qb, qb), 1)

    def idx_chunk(j, carry):
        rmax, rmin = carry
        zw = jnp.maximum(_dot(ki_ref[chunk_rows(j), :], qi_ref[...]), 0.0) * wi_ref[...]
        index = zw[:, 0:qb]
        for h in range(1, IDX_HEADS):
            index = index + zw[:, h * qb:(h + 1) * qb]
        valid = (j * qb + krow) <= (q0 + qcol)
        x = jnp.where(valid, index, NEG)
        idx_ref[chunk_rows(j), :] = x
        rmax = jnp.maximum(rmax, jnp.max(x, axis=0, keepdims=True))
        rmin = jnp.minimum(rmin, jnp.min(jnp.where(valid, index, -NEG), axis=0, keepdims=True))
        return rmax, rmin

    rmax, rmin = lax.fori_loop(
        0, nch, idx_chunk, (jnp.full((1, qb), NEG, F32), jnp.full((1, qb), -NEG, F32)))

    @pl.when((nch & 1) == 1)
    def _():
        idx_ref[chunk_rows(nch), :] = jnp.full((qb, qb), NEG, F32)

    def slabs(j):
        return idx_ref[chunk_rows(j), :].reshape(ns, SUBLANES, qb)

    def col_sum(c8):
        return jnp.sum(c8, axis=0, keepdims=True)

    def count_ge(thr):
        thr8 = jnp.broadcast_to(thr, (SUBLANES, qb))

        def body(jp, accs):
            pair = idx_ref[pl.ds(pl.multiple_of(jp * (2 * qb), 2 * qb), 2 * qb), :]
            accs = list(accs)
            for s in range(2 * ns):
                xs = pair[s * SUBLANES:(s + 1) * SUBLANES]
                accs[s % COUNT_ACCS] = accs[s % COUNT_ACCS] + jnp.where(xs >= thr8, 1.0, 0.0)
            return tuple(accs)

        zero = jnp.zeros((SUBLANES, qb), F32)
        accs = lax.fori_loop(0, (nch + 1) >> 1, body, (zero,) * COUNT_ACCS)
        return col_sum(functools.reduce(lambda a, b: a + b, accs))

    def bracket_extremes(lo, hi):
        lo8 = jnp.broadcast_to(lo, (SUBLANES, qb))[None]
        hi8 = jnp.broadcast_to(hi, (SUBLANES, qb))[None]

        def body(j, carry):
            bmax, bmin = carry
            xs = slabs(j)
            inb = (xs >= lo8) & (xs < hi8)
            bmax = jnp.maximum(bmax, jnp.max(jnp.where(inb, xs, NEG), axis=0))
            bmin = jnp.minimum(bmin, jnp.min(jnp.where(inb, xs, -NEG), axis=0))
            return bmax, bmin

        bmax, bmin = lax.fori_loop(
            0, nch, body,
            (jnp.full((SUBLANES, qb), NEG, F32), jnp.full((SUBLANES, qb), -NEG, F32)))
        return jnp.max(bmax, axis=0, keepdims=True), jnp.min(bmin, axis=0, keepdims=True)

    def resolve_small_brackets(lo, hi, c_lo, c_hi, tie):
        bmax, bmin = bracket_extremes(lo, hi)
        is_open = c_lo > kf
        top_only = is_open & (bmax != bmin) & ((kf - c_hi) == 1.0)
        lo = jnp.where(top_only, bmax, lo)
        tie = jnp.where(is_open & ((bmax == bmin) | top_only), 1.0, tie)
        return lo, tie

    nvalid = (q0 + 1 + lax.broadcasted_iota(jnp.int32, (1, qb), 1)).astype(F32)
    hi0 = rmax + (jnp.abs(rmax) * 1e-6 + 1e-30)

    def open_rows(c_lo, tie):
        return jnp.max(jnp.where((c_lo > kf) & (tie < 0.5), 1.0, 0.0)) > 0.0

    def search_cond(state):
        it, _, _, c_lo, _, tie = state
        return (it < SEARCH_MAX_ITERS) & open_rows(c_lo, tie)

    def search_body(state):
        it, lo, hi, c_lo, c_hi, tie = state
        for _ in range(SEARCH_UNROLL):
            mid = 0.5 * (lo + hi)
            c = count_ge(mid)
            ok = c >= kf
            lo, hi = jnp.where(ok, mid, lo), jnp.where(ok, hi, mid)
            c_lo, c_hi = jnp.where(ok, c, c_lo), jnp.where(ok, c_hi, c)
        it = it + SEARCH_UNROLL
        check = (it >= VERIFY_FROM) & (((it - VERIFY_FROM) & (VERIFY_EVERY - 1)) == 0)
        lo, tie = lax.cond(check, lambda: resolve_small_brackets(lo, hi, c_lo, c_hi, tie),
                           lambda: (lo, tie))
        return it, lo, hi, c_lo, c_hi, tie

    _, lo, hi, c_lo, c_hi, _ = lax.while_loop(
        search_cond, search_body,
        (jnp.int32(0), rmin, hi0, nvalid, jnp.zeros((1, qb), F32), jnp.zeros((1, qb), F32)))

    excess = c_lo > kf

    @pl.when(jnp.max(jnp.where(excess, 1.0, 0.0)) > 0.0)
    def _():
        need = jnp.where(excess, kf - c_hi, float(seq + 1))
        tri = jnp.where(lax.broadcasted_iota(jnp.int32, (qb, qb), 1)
                        <= lax.broadcasted_iota(jnp.int32, (qb, qb), 0), 1.0, 0.0).astype(BF16)

        def demote(j, seen):
            xs = idx_ref[chunk_rows(j), :]
            inb = (xs >= lo) & (xs < hi)
            inb_f = jnp.where(inb, 1.0, 0.0)
            rank = seen + _dot(tri, inb_f.astype(BF16))
            idx_ref[chunk_rows(j), :] = jnp.where(inb & (rank > need), NEG, xs)
            return seen + jnp.sum(inb_f, axis=0, keepdims=True)

        lax.fori_loop(0, nch, demote, jnp.zeros((1, qb), F32))

    m_ref[...] = jnp.full(m_ref.shape, MASK, F32)
    l_ref[...] = jnp.zeros(l_ref.shape, F32)
    acc_ref[...] = jnp.zeros(acc_ref.shape, F32)

    def att_step(j, bias_ref):
        ckc = ckv_ref[chunk_rows(j), :]
        ckc_t = ckvt_ref[:, chunk_rows(j)]
        madd = jnp.where(idx_ref[chunk_rows(j), :] >= lo, 0.0, MASK)
        s_all = _dot(ckc, qlat_ref[...])
        s_h = []
        for h in range(N_HEADS):
            s = s_all[:, h * qb:(h + 1) * qb] + madd
            if bias_ref is not None:
                s = s + bias_ref[:, h * qb:(h + 1) * qb]
            s_h.append(s)
        s_all = jnp.concatenate(s_h, axis=1)
        m_old = m_ref[...]
        m_new = jnp.maximum(m_old, jnp.max(s_all, axis=0, keepdims=True))
        alpha = jnp.exp2(m_old - m_new)
        p = jnp.exp2(s_all - m_new)
        l_ref[...] = alpha * l_ref[...] + jnp.sum(p, axis=0, keepdims=True)
        acc_ref[...] = alpha * acc_ref[...] + _dot(ckc_t, p.astype(BF16))
        m_ref[...] = m_new

    def far_pair(jp, carry):
        att_step(2 * jp, None)
        att_step(2 * jp + 1, None)
        return carry

    nfar = jnp.maximum(nch - 2, 0)
    lax.fori_loop(0, nfar >> 1, far_pair, 0)

    @pl.when((nfar & 1) == 1)
    def _():
        att_step(nfar - 1, None)

    @pl.when(nch >= 2)
    def _():
        att_step(nch - 2, d1_ref)

    att_step(nch - 1, d0_ref)

    o_ref[...] = (acc_ref[...] / l_ref[...]).astype(BF16)


def _attention(qlat_w, qi_w, wi_w, ckv, ckv_t, ki, d0, d1, batch, seq):
    qb = Q_BLK
    nq = seq // qb
    n = batch * seq
    topk = min(TOPK_MAX, seq // 4)
    kern = functools.partial(_attn_kernel, topk=topk, seq=seq)
    wide = lambda h: pl.BlockSpec((h, N_HEADS * qb), lambda b, j: (0, b * nq + j))
    return pl.pallas_call(
        kern,
        grid=(batch, nq),
        in_specs=[
            wide(KV_RANK), wide(IDX_DIM), wide(1),
            pl.BlockSpec((seq, KV_RANK), lambda b, j: (b, 0)),
            pl.BlockSpec((KV_RANK, seq), lambda b, j: (0, b)),
            pl.BlockSpec((seq, IDX_DIM), lambda b, j: (b, 0)),
            _const_spec(d0.shape), _const_spec(d1.shape),
        ],
        out_specs=wide(KV_RANK),
        out_shape=jax.ShapeDtypeStruct((KV_RANK, N_HEADS * n), BF16),
        scratch_shapes=[
            pltpu.VMEM((seq + qb, qb), F32),
            pltpu.VMEM((1, N_HEADS * qb), F32),
            pltpu.VMEM((1, N_HEADS * qb), F32),
            pltpu.VMEM((KV_RANK, N_HEADS * qb), F32),
        ],
        compiler_params=pltpu.CompilerParams(
            dimension_semantics=("arbitrary", "arbitrary"), vmem_limit_bytes=VMEM_LIMIT),
        name="sparse_attention",
    )(qlat_w, qi_w, wi_w, ckv, ckv_t, ki, d0, d1)


def _mix_kernel(x_ref, ol_ref, u_ref, v_ref, ga_ref, gb_ref, wuv_ref, wpa_ref, ws_ref, bs_ref,
                wpb_ref, wout_ref, g2_ref, wr_ref, br_ref, x1_ref, h2_ref, comb_ref, goh_ref):
    tm = x_ref.shape[0]
    o_at = jnp.concatenate(
        [_dot(wuv_ref[h], ol_ref[:, h * tm:(h + 1) * tm]) for h in range(N_HEADS)], axis=0)
    y_a = _dot(jnp.transpose(o_at).astype(BF16), wpa_ref[...])
    v = v_ref[...]
    u = u_ref[...].astype(F32)
    rows = []
    for c in range(tm // CHUNK):
        cols = []
        for g in range(GMLP_GROUPS):
            vc = v[c * CHUNK:(c + 1) * CHUNK, g * GMLP_GROUP_DIM:(g + 1) * GMLP_GROUP_DIM]
            cols.append(_dot(ws_ref[g], vc) + bs_ref[g])
        rows.append(jnp.concatenate(cols, axis=1))
    s = jnp.concatenate(rows, axis=0)
    y_b = _dot((u * s).astype(BF16), wpb_ref[...])
    merged = ga_ref[...].astype(F32) * y_a + gb_ref[...].astype(F32) * y_b
    x1 = x_ref[...] + _dot(merged.astype(BF16), wout_ref[...])
    x1_ref[...] = x1
    h2 = _rms(x1, g2_ref[...])
    h2_hi = h2.astype(BF16)
    h2_ref[...] = h2_hi

    h2_lo = (h2 - h2_hi.astype(F32)).astype(BF16)
    logits = (_dot(h2_hi, wr_ref[0]) + (_dot(h2_lo, wr_ref[0]) + _dot(h2_hi, wr_ref[1]))
              + br_ref[...])
    lane = lax.broadcasted_iota(jnp.int32, logits.shape, 1)
    big = jnp.int32(4 * LANES)
    is_g = (lane >= N_EXPERTS) & (lane < N_EXPERTS + N_GROUPS)
    lg = jnp.where(is_g, logits, NEG)
    gmax = jnp.max(lg, axis=1, keepdims=True)
    g_w = 1.0 / jnp.sum(jnp.where(is_g, jnp.exp(lg - gmax), 0.0), axis=1, keepdims=True)
    g_idx = jnp.min(jnp.where(is_g & (lg == gmax), lane, big), axis=1, keepdims=True) - N_EXPERTS
    in_g = (lane < N_EXPERTS) & ((lane >> 3) == g_idx)
    le = jnp.where(in_g, logits, NEG)
    emax = jnp.max(le, axis=1, keepdims=True)
    ee = jnp.where(in_g, jnp.exp(le - emax), 0.0)
    ep = ee / jnp.sum(ee, axis=1, keepdims=True)
    ep1 = jnp.where(in_g, ep, -1.0)
    m1 = jnp.max(ep1, axis=1, keepdims=True)
    i1 = jnp.min(jnp.where(ep1 == m1, lane, big), axis=1, keepdims=True)
    ep2 = jnp.where(lane == i1, -1.0, ep1)
    m2 = jnp.max(ep2, axis=1, keepdims=True)
    i2 = jnp.min(jnp.where(ep2 == m2, lane, big), axis=1, keepdims=True)
    tot = m1 + m2
    comb_ref[...] = (jnp.where(lane == i1, g_w * m1 / tot, 0.0)
                     + jnp.where(lane == i2, g_w * m2 / tot, 0.0))
    goh_ref[...] = jnp.transpose(jnp.where(lane == g_idx, 1.0, 0.0))[0:SUBLANES, :]


def _mix(x2, olat_t, u, v, ga, gb, wuv, wpa, ws, bs, wpb, wout, g2, wr, br):
    n, d = x2.shape
    tm = TM_MIX
    row = lambda w: pl.BlockSpec((tm, w), lambda i: (i, 0))
    weights = (wuv, wpa, ws, bs, wpb, wout, g2, wr, br)
    return pl.pallas_call(
        _mix_kernel,
        grid=(n // tm,),
        in_specs=[row(d), pl.BlockSpec((KV_RANK, N_HEADS * tm), lambda i: (0, i)),
                  row(GMLP_WIDTH), row(GMLP_WIDTH), row(d), row(d)]
        + [_const_spec(w.shape) for w in weights],
        out_specs=(row(d), row(d), row(LANES), pl.BlockSpec((SUBLANES, tm), lambda i: (0, i))),
        out_shape=(jax.ShapeDtypeStruct((n, d), F32), jax.ShapeDtypeStruct((n, d), BF16),
                   jax.ShapeDtypeStruct((n, LANES), F32),
                   jax.ShapeDtypeStruct((SUBLANES, n), F32)),
        compiler_params=pltpu.CompilerParams(
            dimension_semantics=("arbitrary",), vmem_limit_bytes=VMEM_LIMIT),
        name="merge_router",
    )(x2, olat_t, u, v, ga, gb, *weights)


def _moe_kernel(x1_ref, h2_ref, comb_ref, goh_ref, tri_ref, wg_ref, wu_ref, wd_ref, gf_ref, o_ref,
                perm_ref, pos_ref, hs_ref, cs_ref, ys_ref, seg_ref):
    step = pl.program_id(1)
    tt, d = hs_ref.shape
    rb = MOE_ROWS
    pc = MOE_PERM_ROWS

    @pl.when(step == 0)
    def _():
        goh = goh_ref[...]
        before = _dot(goh.astype(BF16), tri_ref[...])
        pos = jnp.zeros((1, tt), F32)
        start = jnp.float32(0.0)
        for g in range(N_GROUPS):
            cnt = jnp.sum(goh[g:g + 1, :])
            seg_ref[g] = jnp.floor(start * (1.0 / rb)).astype(jnp.int32)
            seg_ref[N_GROUPS + g] = jnp.ceil((start + cnt) * (1.0 / rb)).astype(jnp.int32)
            pos = pos + goh[g:g + 1, :] * (start + before[g:g + 1, :])
            start = start + cnt
        pos_ref[...] = pos
        for r in range(tt // pc):
            dst = (lax.broadcasted_iota(jnp.int32, (pc, tt), 0) + r * pc).astype(F32)
            perm_ref[r * pc:(r + 1) * pc, :] = jnp.where(pos == dst, 1.0, 0.0).astype(BF16)
        perm = perm_ref[...]
        hs_ref[...] = _dot(perm, h2_ref[...]).astype(BF16)
        comb = comb_ref[...]
        comb_hi = comb.astype(BF16)
        comb_lo = (comb - comb_hi.astype(F32)).astype(BF16)
        cs_ref[...] = _dot(perm, comb_hi) + _dot(perm, comb_lo)
        ys_ref[...] = jnp.zeros(ys_ref.shape, F32)

    group = step // (EXPERTS_PER_GROUP // MOE_EXPERTS_PER_STEP)
    first = seg_ref[group]
    last = seg_ref[N_GROUPS + group]

    def block(b, carry):
        rows = pl.ds(pl.multiple_of(b * rb, rb), rb)
        hb = hs_ref[rows, :]
        cb = cs_ref[rows, :]
        lane = lax.broadcasted_iota(jnp.int32, cb.shape, 1)
        y = jnp.zeros((rb, d), F32)
        for k in range(MOE_EXPERTS_PER_STEP):
            e = step * MOE_EXPERTS_PER_STEP + k
            w_e = jnp.sum(jnp.where(lane == e, cb, 0.0), axis=1, keepdims=True)
            act = jax.nn.silu(_dot(hb, wg_ref[k])) * _dot(hb, wu_ref[k]) * w_e
            y = y + _dot(act.astype(BF16), wd_ref[k])
        ys_ref[rows, :] += y
        return carry

    lax.fori_loop(first, last, block, 0)

    @pl.when(step == pl.num_programs(1) - 1)
    def _():
        ys = ys_ref[...].astype(BF16)
        pos_col = jnp.transpose(jnp.broadcast_to(pos_ref[...], (SUBLANES, tt)))[:, 0:1]
        for r in range(tt // pc):
            rows = slice(r * pc, (r + 1) * pc)
            src = lax.broadcasted_iota(jnp.int32, (pc, tt), 1).astype(F32)
            unperm = jnp.where(pos_col[rows] == src, 1.0, 0.0).astype(BF16)
            o_ref[rows, :] = _rms(x1_ref[rows, :] + _dot(unperm, ys), gf_ref[...])


def _moe(x1, h2, comb, goh_t, wg, wu, wd, gf):
    n, d = x1.shape
    tt = min(TM_MOE, n)
    eps = MOE_EXPERTS_PER_STEP
    assert EXPERTS_PER_GROUP % eps == 0 and tt % MOE_ROWS == 0 and tt % MOE_PERM_ROWS == 0
    row = lambda w: pl.BlockSpec((tt, w), lambda i, s: (i, 0))
    tri = (jnp.arange(tt)[:, None] < jnp.arange(tt)[None, :]).astype(BF16)
    return pl.pallas_call(
        _moe_kernel,
        grid=(n // tt, N_EXPERTS // eps),
        in_specs=[row(d), row(d), row(LANES),
                  pl.BlockSpec((SUBLANES, tt), lambda i, s: (0, i)),
                  _const_spec(tri.shape),
                  pl.BlockSpec((eps, d, D_FF_EXPERT), lambda i, s: (s, 0, 0)),
                  pl.BlockSpec((eps, d, D_FF_EXPERT), lambda i, s: (s, 0, 0)),
                  pl.BlockSpec((eps, D_FF_EXPERT, d), lambda i, s: (s, 0, 0)),
                  _const_spec(gf.shape)],
        out_specs=row(d),
        out_shape=jax.ShapeDtypeStruct((n, d), F32),
        scratch_shapes=[
            pltpu.VMEM((tt, tt), BF16),
            pltpu.VMEM((1, tt), F32),
            pltpu.VMEM((tt, d), BF16),
            pltpu.VMEM((tt, LANES), F32),
            pltpu.VMEM((tt, d), F32),
            pltpu.SMEM((2 * N_GROUPS,), jnp.int32),
        ],
        compiler_params=pltpu.CompilerParams(
            dimension_semantics=("arbitrary", "arbitrary"), vmem_limit_bytes=VMEM_LIMIT),
        name="experts",
    )(x1, h2, comb, goh_t, tri, wg, wu, wd, gf)


def _t5_bucket(n):
    max_exact = NUM_BUCKETS // 2
    nf = jnp.maximum(n, 1).astype(F32)
    large = max_exact + jnp.floor(
        jnp.log(nf / max_exact) / np.float32(np.log(MAX_DISTANCE / max_exact))
        * (NUM_BUCKETS - max_exact)).astype(jnp.int32)
    large = jnp.minimum(large, NUM_BUCKETS - 1)
    return jnp.where(n < max_exact, n, large)


def _bias_tiles(rel_bias):
    qb = Q_BLK
    k = jnp.arange(qb, dtype=jnp.int32)[:, None]
    q = jnp.arange(qb, dtype=jnp.int32)[None, :]
    rel = (rel_bias - rel_bias[NUM_BUCKETS - 1]) * LOG2E

    def tile(dist):
        onehot = (_t5_bucket(dist)[:, :, None] == jnp.arange(NUM_BUCKETS)).astype(F32)
        t = jnp.einsum("kqb,bh->khq", onehot, rel, precision=lax.Precision.HIGHEST)
        return t.reshape(qb, N_HEADS * qb)

    return tile(jnp.maximum(q - k, 0)), tile(qb + q - k)


def _block_diag(w):
    h, a, b = w.shape
    eye = jnp.eye(h, dtype=w.dtype)
    return (eye[:, None, :, None] * w[:, :, None, :]).reshape(h * a, h * b)


def kernel(x, w_in, kv_norm_g, w_uk, w_uv, rel_bias, ln_v_g, ln_v_b, w_spatial, b_spatial,
           w_proj_a, w_proj_b, w_out, norm1_g, norm2_g, router_group_w, router_group_b,
           router_expert_w, router_expert_b, w_gate, w_up, w_down, final_norm_g):
    batch, seq, d = x.shape
    depth = w_in.shape[0]
    n = batch * seq
    assert depth == 1, "the final rms-norm is fused into the (single) layer's expert kernel"
    assert EXPERTS_PER_GROUP == 8
    assert seq % Q_BLK == 0 and n % min(TM_MOE, n) == 0
    assert TM_PROJ == Q_BLK == TM_MIX
    assert Q_BLK + 1 >= MAX_DISTANCE

    attn_w = N_HEADS * HEAD_DIM
    sizes = (attn_w, KV_RANK, IDX_HEADS * IDX_DIM, IDX_DIM, IDX_HEADS, GMLP_WIDTH, GMLP_WIDTH, d, d)
    offs = np.concatenate([[0], np.cumsum(sizes)])
    d0, d1 = _bias_tiles(rel_bias)
    tril = jnp.tril(jnp.ones((CHUNK, CHUNK), dtype=bool))
    row = lambda a: a.reshape(1, -1)

    x2 = x.reshape(n, d)
    l = 0
    wl = w_in[l].astype(BF16)
    wq, wc, wqi, wki, wwi, wu, wv, wga, wgb = [wl[:, offs[i]:offs[i + 1]] for i in range(9)]
    wt = jnp.concatenate([wq, wqi, wc, wwi], axis=1).T
    wuk_bd = _block_diag(jnp.transpose(w_uk[l], (1, 0, 2))).astype(BF16)
    wuv_t = jnp.transpose(w_uv[l], (1, 2, 0)).astype(BF16)

    qlat_t, qi_t, wi_t, ckv_t, ckv, ki, u, v, ga, gb = _projection(
        x2, row(norm1_g[l]), wt, wuk_bd, kv_norm_g[l].reshape(-1, 1), wc, row(kv_norm_g[l]),
        wki, wu, wv, row(ln_v_g[l]), row(ln_v_b[l]), wga, wgb)

    olat_t = _attention(qlat_t, qi_t, wi_t, ckv, ckv_t, ki, d0, d1, batch, seq)

    ws = jnp.where(tril[None], w_spatial[l], 0.0).astype(BF16)
    bs = jnp.broadcast_to(b_spatial[l][:, :, None], (GMLP_GROUPS, CHUNK, GMLP_GROUP_DIM))
    wr = jnp.concatenate(
        [router_expert_w[l], router_group_w[l],
         jnp.zeros((d, LANES - N_EXPERTS - N_GROUPS), F32)], axis=1)
    wr_hi = wr.astype(BF16)
    wr = jnp.stack([wr_hi, (wr - wr_hi.astype(F32)).astype(BF16)])
    br = jnp.concatenate(
        [router_expert_b[l], router_group_b[l],
         jnp.zeros((LANES - N_EXPERTS - N_GROUPS,), F32)]).reshape(1, LANES)
    x1, h2, comb, goh_t = _mix(
        x2, olat_t, u, v, ga, gb, wuv_t, w_proj_a[l].astype(BF16), ws, bs,
        w_proj_b[l].astype(BF16), w_out[l].astype(BF16), row(norm2_g[l]), wr, br)

    out = _moe(x1, h2, comb, goh_t, w_gate[l].astype(BF16), w_up[l].astype(BF16),
               w_down[l].astype(BF16), row(final_norm_g))
    return out.reshape(batch, seq, d)
```

```python
import functools

import numpy as np
import jax
import jax.numpy as jnp
from jax import lax
from jax.experimental import pallas as pl
from jax.experimental.pallas import tpu as pltpu

F32 = jnp.float32
BF16 = jnp.bfloat16

N_HEADS = 8
HEAD_DIM = 64
KV_RANK = 128
IDX_HEADS = 8
IDX_DIM = 64
TOPK_MAX = 256
CHUNK = 128
GMLP_GROUPS = 4
GMLP_GROUP_DIM = 128
GMLP_WIDTH = GMLP_GROUPS * GMLP_GROUP_DIM
NUM_BUCKETS = 32
MAX_DISTANCE = 128
N_GROUPS = 4
EXPERTS_PER_GROUP = 8
N_EXPERTS = N_GROUPS * EXPERTS_PER_GROUP
D_FF_EXPERT = 256
EPS = 1e-6

CKVT_ROWS = KV_RANK + 16
LANES = 128
SUBLANES = 8
VMEM_LIMIT = 52 * 1024 * 1024
NEG = -3.0e38
MASK = -1.0e30

TM_PROJ = 256
Q_BLK = 256
TM_MIX = 256
TM_MOE = 1024
MOE_ROWS = 128
MOE_PERM_ROWS = 256
MOE_EXPERTS_PER_STEP = 4
SEARCH_MAX_ITERS = 64
SEARCH_UNROLL = 2
COUNT_ACCS = 4
VERIFY_FROM = 16
VERIFY_EVERY = 4
LOG2E = 1.4426950408889634


def _dot(a, b):
    return jnp.dot(a, b, preferred_element_type=F32)


def _dot_t(a, b):
    return lax.dot_general(a, b, (((1,), (1,)), ((), ())), preferred_element_type=F32)


def _rms(x, g):
    return x * lax.rsqrt(jnp.mean(x * x, axis=-1, keepdims=True) + EPS) * g


_T_ROWS = (N_HEADS * HEAD_DIM, IDX_HEADS * IDX_DIM, KV_RANK, IDX_HEADS)
_T_OFFS = tuple(int(v) for v in np.concatenate([[0], np.cumsum(_T_ROWS)]))


def _proj_kernel(x_ref, g1_ref, wt_ref, wuk_ref, gkvc_ref, wc_ref, gkv_ref, wki_ref, wu_ref,
                 wv_ref, lng_ref, lnb_ref, wga_ref, wgb_ref,
                 qlat_ref, qi_ref, wi_ref, ckvt_ref, ckv_ref, ki_ref, u_ref, v_ref, ga_ref, gb_ref):
    hb = _rms(x_ref[...], g1_ref[...]).astype(BF16)
    t_all = _dot_t(wt_ref[...], hb)
    q_t, qi_t, c_t, w_t = [t_all[_T_OFFS[i]:_T_OFFS[i + 1]] for i in range(4)]
    qlat_t = (_dot(wuk_ref[...], q_t.astype(BF16)) * ((HEAD_DIM ** -0.5) * LOG2E)).astype(BF16)
    w_t = w_t * ((IDX_HEADS ** -0.5) * (IDX_DIM ** -0.5))
    tm = x_ref.shape[0]
    for h in range(N_HEADS):
        cols = slice(h * tm, (h + 1) * tm)
        qlat_ref[:, cols] = qlat_t[h * KV_RANK:(h + 1) * KV_RANK]
        qi_ref[:, cols] = qi_t[h * IDX_DIM:(h + 1) * IDX_DIM].astype(BF16)
        wi_ref[:, cols] = w_t[h:h + 1]
    c_n = c_t * lax.rsqrt(jnp.mean(c_t * c_t, axis=0, keepdims=True) + EPS) * gkvc_ref[...]
    ckvt_ref[...] = jnp.concatenate(
        [c_n, jnp.ones((1, tm), F32), jnp.zeros((CKVT_ROWS - KV_RANK - 1, tm), F32)],
        axis=0).astype(BF16)
    ckv_ref[...] = _rms(_dot(hb, wc_ref[...]), gkv_ref[...]).astype(BF16)
    ki_ref[...] = _dot(hb, wki_ref[...]).astype(BF16)
    u_ref[...] = jax.nn.gelu(_dot(hb, wu_ref[...])).astype(BF16)
    v = jax.nn.gelu(_dot(hb, wv_ref[...]))
    mu = jnp.mean(v, axis=-1, keepdims=True)
    var = jnp.mean(jnp.square(v - mu), axis=-1, keepdims=True)
    v_ref[...] = ((v - mu) * lax.rsqrt(var + EPS) * lng_ref[...] + lnb_ref[...]).astype(BF16)
    ga_ref[...] = jax.nn.sigmoid(_dot(hb, wga_ref[...])).astype(BF16)
    gb_ref[...] = jax.nn.sigmoid(_dot(hb, wgb_ref[...])).astype(BF16)


def _const_spec(shape):
    nd = len(shape)
    return pl.BlockSpec(shape, lambda *_: (0,) * nd)


def _projection(x2, g1, wt, wuk, gkvc, wc, gkv, wki, wu, wv, lng, lnb, wga, wgb):
    n, d = x2.shape
    tm = TM_PROJ
    row = lambda w: pl.BlockSpec((tm, w), lambda i: (i, 0))
    col = lambda h: pl.BlockSpec((h, tm), lambda i: (0, i))
    wide = lambda h: pl.BlockSpec((h, N_HEADS * tm), lambda i: (0, i))
    weights = (g1, wt, wuk, gkvc, wc, gkv, wki, wu, wv, lng, lnb, wga, wgb)
    out_shape = (
        jax.ShapeDtypeStruct((KV_RANK, N_HEADS * n), BF16),
        jax.ShapeDtypeStruct((IDX_DIM, IDX_HEADS * n), BF16),
        jax.ShapeDtypeStruct((1, IDX_HEADS * n), F32),
        jax.ShapeDtypeStruct((CKVT_ROWS, n), BF16),
        jax.ShapeDtypeStruct((n, KV_RANK), BF16),
        jax.ShapeDtypeStruct((n, IDX_DIM), BF16),
        jax.ShapeDtypeStruct((n, GMLP_WIDTH), BF16),
        jax.ShapeDtypeStruct((n, GMLP_WIDTH), BF16),
        jax.ShapeDtypeStruct((n, d), BF16),
        jax.ShapeDtypeStruct((n, d), BF16),
    )
    out_specs = (
        wide(KV_RANK), wide(IDX_DIM), wide(1), col(CKVT_ROWS),
        row(KV_RANK), row(IDX_DIM), row(GMLP_WIDTH), row(GMLP_WIDTH), row(d), row(d),
    )
    return pl.pallas_call(
        _proj_kernel,
        grid=(n // tm,),
        in_specs=[row(d)] + [_const_spec(w.shape) for w in weights],
        out_specs=out_specs,
        out_shape=out_shape,
        compiler_params=pltpu.CompilerParams(
            dimension_semantics=("arbitrary",), vmem_limit_bytes=VMEM_LIMIT),
        name="in_projection",
    )(x2, *weights)


def _attn_kernel(qlat_ref, qi_ref, wi_ref, ckv_ref, ckvt_ref, ki_ref, d0_ref, d1_ref, o_ref,
                 idx_ref, m_ref, acc_ref, *, topk, seq):
    qb = Q_BLK
    ns = qb // SUBLANES
    j_q = pl.program_id(1)
    q0 = j_q * qb
    nch = j_q + 1
    kf = float(topk)

    def chunk_rows(j):
        return pl.ds(pl.multiple_of(j * qb, qb), qb)

    krow = lax.broadcasted_iota(jnp.int32, (qb, qb), 0)
    qcol = lax.broadcasted_iota(jnp.int32, (qb, qb), 1)

    def idx_chunk(j, carry):
        rmax, rmin = carry
        zw = jnp.maximum(_dot(ki_ref[chunk_rows(j), :], qi_ref[...]), 0.0) * wi_ref[...]
        index = zw[:, 0:qb]
        for h in range(1, IDX_HEADS):
            index = index + zw[:, h * qb:(h + 1) * qb]
        valid = (j * qb + krow) <= (q0 + qcol)
        x = jnp.where(valid, index, NEG)
        idx_ref[chunk_rows(j), :] = x
        rmax = jnp.maximum(rmax, jnp.max(x, axis=0, keepdims=True))
        rmin = jnp.minimum(rmin, jnp.min(jnp.where(valid, index, -NEG), axis=0, keepdims=True))
        return rmax, rmin

    rmax, rmin = lax.fori_loop(
        0, nch, idx_chunk, (jnp.full((1, qb), NEG, F32), jnp.full((1, qb), -NEG, F32)))

    @pl.when((nch & 1) == 1)
    def _():
        idx_ref[chunk_rows(nch), :] = jnp.full((qb, qb), NEG, F32)

    def slabs(j):
        return idx_ref[chunk_rows(j), :].reshape(ns, SUBLANES, qb)

    def col_sum(c8):
        return jnp.sum(c8, axis=0, keepdims=True)

    def count_ge(thr):
        thr8 = jnp.broadcast_to(thr, (SUBLANES, qb))

        def body(jp, accs):
            pair = idx_ref[pl.ds(pl.multiple_of(jp * (2 * qb), 2 * qb), 2 * qb), :]
            accs = list(accs)
            for s in range(2 * ns):
                xs = pair[s * SUBLANES:(s + 1) * SUBLANES]
                accs[s % COUNT_ACCS] = accs[s % COUNT_ACCS] + jnp.where(xs >= thr8, 1.0, 0.0)
            return tuple(accs)

        zero = jnp.zeros((SUBLANES, qb), F32)
        accs = lax.fori_loop(0, (nch + 1) >> 1, body, (zero,) * COUNT_ACCS)
        return col_sum(functools.reduce(lambda a, b: a + b, accs))

    def bracket_extremes(lo, hi):
        lo8 = jnp.broadcast_to(lo, (SUBLANES, qb))[None]
        hi8 = jnp.broadcast_to(hi, (SUBLANES, qb))[None]

        def body(j, carry):
            bmax, bmin = carry
            xs = slabs(j)
            inb = (xs >= lo8) & (xs < hi8)
            bmax = jnp.maximum(bmax, jnp.max(jnp.where(inb, xs, NEG), axis=0))
            bmin = jnp.minimum(bmin, jnp.min(jnp.where(inb, xs, -NEG), axis=0))
            return bmax, bmin

        bmax, bmin = lax.fori_loop(
            0, nch, body,
            (jnp.full((SUBLANES, qb), NEG, F32), jnp.full((SUBLANES, qb), -NEG, F32)))
        return jnp.max(bmax, axis=0, keepdims=True), jnp.min(bmin, axis=0, keepdims=True)

    def resolve_small_brackets(lo, hi, c_lo, c_hi, tie):
        bmax, bmin = bracket_extremes(lo, hi)
        is_open = c_lo > kf
        top_only = is_open & (bmax != bmin) & ((kf - c_hi) == 1.0)
        lo = jnp.where(top_only, bmax, lo)
        tie = jnp.where(is_open & ((bmax == bmin) | top_only), 1.0, tie)
        return lo, tie

    nvalid = (q0 + 1 + lax.broadcasted_iota(jnp.int32, (1, qb), 1)).astype(F32)
    hi0 = rmax + (jnp.abs(rmax) * 1e-6 + 1e-30)

    def open_rows(c_lo, tie):
        return jnp.max(jnp.where((c_lo > kf) & (tie < 0.5), 1.0, 0.0)) > 0.0

    def search_cond(state):
        it, _, _, c_lo, _, tie = state
        return (it < SEARCH_MAX_ITERS) & open_rows(c_lo, tie)

    def search_body(state):
        it, lo, hi, c_lo, c_hi, tie = state
        for _ in range(SEARCH_UNROLL):
            mid = 0.5 * (lo + hi)
            c = count_ge(mid)
            ok = c >= kf
            lo, hi = jnp.where(ok, mid, lo), jnp.where(ok, hi, mid)
            c_lo, c_hi = jnp.where(ok, c, c_lo), jnp.where(ok, c_hi, c)
        it = it + SEARCH_UNROLL
        check = (it >= VERIFY_FROM) & (((it - VERIFY_FROM) & (VERIFY_EVERY - 1)) == 0)
        lo, tie = lax.cond(check, lambda: resolve_small_brackets(lo, hi, c_lo, c_hi, tie),
                           lambda: (lo, tie))
        return it, lo, hi, c_lo, c_hi, tie

    _, lo, hi, c_lo, c_hi, _ = lax.while_loop(
        search_cond, search_body,
        (jnp.int32(0), rmin, hi0, nvalid, jnp.zeros((1, qb), F32), jnp.zeros((1, qb), F32)))

    excess = c_lo > kf

    @pl.when(jnp.max(jnp.where(excess, 1.0, 0.0)) > 0.0)
    def _():
        need = jnp.where(excess, kf - c_hi, float(seq + 1))
        tri = jnp.where(lax.broadcasted_iota(jnp.int32, (qb, qb), 1)
                        <= lax.broadcasted_iota(jnp.int32, (qb, qb), 0), 1.0, 0.0).astype(BF16)

        def demote(j, seen):
            xs = idx_ref[chunk_rows(j), :]
            inb = (xs >= lo) & (xs < hi)
            inb_f = jnp.where(inb, 1.0, 0.0)
            rank = seen + _dot(tri, inb_f.astype(BF16))
            idx_ref[chunk_rows(j), :] = jnp.where(inb & (rank > need), NEG, xs)
            return seen + jnp.sum(inb_f, axis=0, keepdims=True)

        lax.fori_loop(0, nch, demote, jnp.zeros((1, qb), F32))

    m_ref[...] = jnp.full(m_ref.shape, MASK, F32)
    acc_ref[...] = jnp.zeros(acc_ref.shape, F32)

    def att_step(j, bias_ref):
        ckc = ckv_ref[chunk_rows(j), :]
        ckc_t = ckvt_ref[:, chunk_rows(j)]
        madd = jnp.where(idx_ref[chunk_rows(j), :] >= lo, 0.0, MASK)
        s_all = _dot(ckc, qlat_ref[...])
        s_h = []
        for h in range(N_HEADS):
            s = s_all[:, h * qb:(h + 1) * qb] + madd
            if bias_ref is not None:
                s = s + bias_ref[:, h * qb:(h + 1) * qb]
            s_h.append(s)
        s_all = jnp.concatenate(s_h, axis=1)
        m_old = m_ref[...]
        m_new = jnp.maximum(m_old, jnp.max(s_all, axis=0, keepdims=True))
        alpha = jnp.exp2(m_old - m_new)
        p = jnp.exp2((s_all - m_new).astype(BF16))
        acc_ref[...] = alpha * acc_ref[...] + _dot(ckc_t, p)
        m_ref[...] = m_new

    def far_pair(jp, carry):
        att_step(2 * jp, None)
        att_step(2 * jp + 1, None)
        return carry

    nfar = jnp.maximum(nch - 2, 0)
    lax.fori_loop(0, nfar >> 1, far_pair, 0)

    @pl.when((nfar & 1) == 1)
    def _():
        att_step(nfar - 1, None)

    @pl.when(nch >= 2)
    def _():
        att_step(nch - 2, d1_ref)

    att_step(nch - 1, d0_ref)

    o_ref[...] = (acc_ref[0:KV_RANK, :] / acc_ref[KV_RANK:KV_RANK + 1, :]).astype(BF16)


def _attention(qlat_w, qi_w, wi_w, ckv, ckv_t, ki, d0, d1, batch, seq):
    qb = Q_BLK
    nq = seq // qb
    n = batch * seq
    topk = min(TOPK_MAX, seq // 4)
    kern = functools.partial(_attn_kernel, topk=topk, seq=seq)
    wide = lambda h: pl.BlockSpec((h, N_HEADS * qb), lambda b, j: (0, b * nq + j))
    return pl.pallas_call(
        kern,
        grid=(batch, nq),
        in_specs=[
            wide(KV_RANK), wide(IDX_DIM), wide(1),
            pl.BlockSpec((seq, KV_RANK), lambda b, j: (b, 0)),
            pl.BlockSpec((CKVT_ROWS, seq), lambda b, j: (0, b)),
            pl.BlockSpec((seq, IDX_DIM), lambda b, j: (b, 0)),
            _const_spec(d0.shape), _const_spec(d1.shape),
        ],
        out_specs=wide(KV_RANK),
        out_shape=jax.ShapeDtypeStruct((KV_RANK, N_HEADS * n), BF16),
        scratch_shapes=[
            pltpu.VMEM((seq + qb, qb), F32),
            pltpu.VMEM((1, N_HEADS * qb), F32),
            pltpu.VMEM((CKVT_ROWS, N_HEADS * qb), F32),
        ],
        compiler_params=pltpu.CompilerParams(
            dimension_semantics=("arbitrary", "arbitrary"), vmem_limit_bytes=VMEM_LIMIT),
        name="sparse_attention",
    )(qlat_w, qi_w, wi_w, ckv, ckv_t, ki, d0, d1)


def _mix_kernel(x_ref, ol_ref, u_ref, v_ref, ga_ref, gb_ref, wuv_ref, wpa_ref, ws_ref, bs_ref,
                wpb_ref, wout_ref, g2_ref, wr_ref, br_ref, x1_ref, h2_ref, comb_ref, goh_ref):
    tm = x_ref.shape[0]
    o_at = jnp.concatenate(
        [_dot(wuv_ref[h], ol_ref[:, h * tm:(h + 1) * tm]) for h in range(N_HEADS)], axis=0)
    y_a = _dot(jnp.transpose(o_at).astype(BF16), wpa_ref[...])
    v = v_ref[...]
    u = u_ref[...].astype(F32)
    rows = []
    for c in range(tm // CHUNK):
        cols = []
        for g in range(GMLP_GROUPS):
            vc = v[c * CHUNK:(c + 1) * CHUNK, g * GMLP_GROUP_DIM:(g + 1) * GMLP_GROUP_DIM]
            cols.append(_dot(ws_ref[g], vc) + bs_ref[g])
        rows.append(jnp.concatenate(cols, axis=1))
    s = jnp.concatenate(rows, axis=0)
    y_b = _dot((u * s).astype(BF16), wpb_ref[...])
    merged = ga_ref[...].astype(F32) * y_a + gb_ref[...].astype(F32) * y_b
    x1 = x_ref[...] + _dot(merged.astype(BF16), wout_ref[...])
    x1_ref[...] = x1
    h2 = _rms(x1, g2_ref[...])
    h2_hi = h2.astype(BF16)
    h2_ref[...] = h2_hi

    h2_lo = (h2 - h2_hi.astype(F32)).astype(BF16)
    logits = (_dot(h2_hi, wr_ref[0]) + (_dot(h2_lo, wr_ref[0]) + _dot(h2_hi, wr_ref[1]))
              + br_ref[...])
    lane = lax.broadcasted_iota(jnp.int32, logits.shape, 1)
    big = jnp.int32(4 * LANES)
    is_g = (lane >= N_EXPERTS) & (lane < N_EXPERTS + N_GROUPS)
    lg = jnp.where(is_g, logits, NEG)
    gmax = jnp.max(lg, axis=1, keepdims=True)
    g_w = 1.0 / jnp.sum(jnp.where(is_g, jnp.exp(lg - gmax), 0.0), axis=1, keepdims=True)
    g_idx = jnp.min(jnp.where(is_g & (lg == gmax), lane, big), axis=1, keepdims=True) - N_EXPERTS
    in_g = (lane < N_EXPERTS) & ((lane >> 3) == g_idx)
    le = jnp.where(in_g, logits, NEG)
    emax = jnp.max(le, axis=1, keepdims=True)
    ee = jnp.where(in_g, jnp.exp(le - emax), 0.0)
    ep = ee / jnp.sum(ee, axis=1, keepdims=True)
    ep1 = jnp.where(in_g, ep, -1.0)
    m1 = jnp.max(ep1, axis=1, keepdims=True)
    i1 = jnp.min(jnp.where(ep1 == m1, lane, big), axis=1, keepdims=True)
    ep2 = jnp.where(lane == i1, -1.0, ep1)
    m2 = jnp.max(ep2, axis=1, keepdims=True)
    i2 = jnp.min(jnp.where(ep2 == m2, lane, big), axis=1, keepdims=True)
    tot = m1 + m2
    comb_ref[...] = (jnp.where(lane == i1, g_w * m1 / tot, 0.0)
                     + jnp.where(lane == i2, g_w * m2 / tot, 0.0))
    goh_ref[...] = jnp.transpose(jnp.where(lane == g_idx, 1.0, 0.0))[0:SUBLANES, :]


def _mix(x2, olat_t, u, v, ga, gb, wuv, wpa, ws, bs, wpb, wout, g2, wr, br):
    n, d = x2.shape
    tm = TM_MIX
    row = lambda w: pl.BlockSpec((tm, w), lambda i: (i, 0))
    weights = (wuv, wpa, ws, bs, wpb, wout, g2, wr, br)
    return pl.pallas_call(
        _mix_kernel,
        grid=(n // tm,),
        in_specs=[row(d), pl.BlockSpec((KV_RANK, N_HEADS * tm), lambda i: (0, i)),
                  row(GMLP_WIDTH), row(GMLP_WIDTH), row(d), row(d)]
        + [_const_spec(w.shape) for w in weights],
        out_specs=(row(d), row(d), row(LANES), pl.BlockSpec((SUBLANES, tm), lambda i: (0, i))),
        out_shape=(jax.ShapeDtypeStruct((n, d), F32), jax.ShapeDtypeStruct((n, d), BF16),
                   jax.ShapeDtypeStruct((n, LANES), F32),
                   jax.ShapeDtypeStruct((SUBLANES, n), F32)),
        compiler_params=pltpu.CompilerParams(
            dimension_semantics=("arbitrary",), vmem_limit_bytes=VMEM_LIMIT),
        name="merge_router",
    )(x2, olat_t, u, v, ga, gb, *weights)


def _moe_kernel(x1_ref, h2_ref, comb_ref, goh_ref, tri_ref, wg_ref, wu_ref, wd_ref, gf_ref, o_ref,
                perm_ref, pos_ref, hs_ref, cs_ref, ys_ref, seg_ref):
    step = pl.program_id(1)
    tt, d = hs_ref.shape
    rb = MOE_ROWS
    pc = MOE_PERM_ROWS

    @pl.when(step == 0)
    def _():
        goh = goh_ref[...]
        before = _dot(goh.astype(BF16), tri_ref[...])
        pos = jnp.zeros((1, tt), F32)
        start = jnp.float32(0.0)
        for g in range(N_GROUPS):
            cnt = jnp.sum(goh[g:g + 1, :])
            seg_ref[g] = jnp.floor(start * (1.0 / rb)).astype(jnp.int32)
            seg_ref[N_GROUPS + g] = jnp.ceil((start + cnt) * (1.0 / rb)).astype(jnp.int32)
            pos = pos + goh[g:g + 1, :] * (start + before[g:g + 1, :])
            start = start + cnt
        pos_ref[...] = pos
        for r in range(tt // pc):
            dst = (lax.broadcasted_iota(jnp.int32, (pc, tt), 0) + r * pc).astype(F32)
            perm_ref[r * pc:(r + 1) * pc, :] = jnp.where(pos == dst, 1.0, 0.0).astype(BF16)
        perm = perm_ref[...]
        hs_ref[...] = _dot(perm, h2_ref[...]).astype(BF16)
        comb = comb_ref[...]
        comb_hi = comb.astype(BF16)
        comb_lo = (comb - comb_hi.astype(F32)).astype(BF16)
        cs_ref[...] = _dot(perm, comb_hi) + _dot(perm, comb_lo)
        ys_ref[...] = jnp.zeros(ys_ref.shape, F32)

    group = step // (EXPERTS_PER_GROUP // MOE_EXPERTS_PER_STEP)
    first = seg_ref[group]
    last = seg_ref[N_GROUPS + group]

    def block(b, carry):
        rows = pl.ds(pl.multiple_of(b * rb, rb), rb)
        hb = hs_ref[rows, :]
        cb = cs_ref[rows, :]
        lane = lax.broadcasted_iota(jnp.int32, cb.shape, 1)
        y = jnp.zeros((rb, d), F32)
        for k in range(MOE_EXPERTS_PER_STEP):
            e = step * MOE_EXPERTS_PER_STEP + k
            w_e = jnp.sum(jnp.where(lane == e, cb, 0.0), axis=1, keepdims=True)
            act = jax.nn.silu(_dot(hb, wg_ref[k])) * _dot(hb, wu_ref[k]) * w_e
            y = y + _dot(act.astype(BF16), wd_ref[k])
        ys_ref[rows, :] += y
        return carry

    lax.fori_loop(first, last, block, 0)

    @pl.when(step == pl.num_programs(1) - 1)
    def _():
        ys = ys_ref[...].astype(BF16)
        pos_col = jnp.transpose(jnp.broadcast_to(pos_ref[...], (SUBLANES, tt)))[:, 0:1]
        for r in range(tt // pc):
            rows = slice(r * pc, (r + 1) * pc)
            src = lax.broadcasted_iota(jnp.int32, (pc, tt), 1).astype(F32)
            unperm = jnp.where(pos_col[rows] == src, 1.0, 0.0).astype(BF16)
            o_ref[rows, :] = _rms(x1_ref[rows, :] + _dot(unperm, ys), gf_ref[...])


def _moe(x1, h2, comb, goh_t, wg, wu, wd, gf):
    n, d = x1.shape
    tt = min(TM_MOE, n)
    eps = MOE_EXPERTS_PER_STEP
    assert EXPERTS_PER_GROUP % eps == 0 and tt % MOE_ROWS == 0 and tt % MOE_PERM_ROWS == 0
    row = lambda w: pl.BlockSpec((tt, w), lambda i, s: (i, 0))
    tri = (jnp.arange(tt)[:, None] < jnp.arange(tt)[None, :]).astype(BF16)
    return pl.pallas_call(
        _moe_kernel,
        grid=(n // tt, N_EXPERTS // eps),
        in_specs=[row(d), row(d), row(LANES),
                  pl.BlockSpec((SUBLANES, tt), lambda i, s: (0, i)),
                  _const_spec(tri.shape),
                  pl.BlockSpec((eps, d, D_FF_EXPERT), lambda i, s: (s, 0, 0)),
                  pl.BlockSpec((eps, d, D_FF_EXPERT), lambda i, s: (s, 0, 0)),
                  pl.BlockSpec((eps, D_FF_EXPERT, d), lambda i, s: (s, 0, 0)),
                  _const_spec(gf.shape)],
        out_specs=row(d),
        out_shape=jax.ShapeDtypeStruct((n, d), F32),
        scratch_shapes=[
            pltpu.VMEM((tt, tt), BF16),
            pltpu.VMEM((1, tt), F32),
            pltpu.VMEM((tt, d), BF16),
            pltpu.VMEM((tt, LANES), F32),
            pltpu.VMEM((tt, d), F32),
            pltpu.SMEM((2 * N_GROUPS,), jnp.int32),
        ],
        compiler_params=pltpu.CompilerParams(
            dimension_semantics=("arbitrary", "arbitrary"), vmem_limit_bytes=VMEM_LIMIT),
        name="experts",
    )(x1, h2, comb, goh_t, tri, wg, wu, wd, gf)


def _t5_bucket(n):
    max_exact = NUM_BUCKETS // 2
    nf = jnp.maximum(n, 1).astype(F32)
    large = max_exact + jnp.floor(
        jnp.log(nf / max_exact) / np.float32(np.log(MAX_DISTANCE / max_exact))
        * (NUM_BUCKETS - max_exact)).astype(jnp.int32)
    large = jnp.minimum(large, NUM_BUCKETS - 1)
    return jnp.where(n < max_exact, n, large)


def _bias_tiles(rel_bias):
    qb = Q_BLK
    k = jnp.arange(qb, dtype=jnp.int32)[:, None]
    q = jnp.arange(qb, dtype=jnp.int32)[None, :]
    rel = (rel_bias - rel_bias[NUM_BUCKETS - 1]) * LOG2E

    def tile(dist):
        onehot = (_t5_bucket(dist)[:, :, None] == jnp.arange(NUM_BUCKETS)).astype(F32)
        t = jnp.einsum("kqb,bh->khq", onehot, rel, precision=lax.Precision.HIGHEST)
        return t.reshape(qb, N_HEADS * qb)

    return tile(jnp.maximum(q - k, 0)), tile(qb + q - k)


def _block_diag(w):
    h, a, b = w.shape
    eye = jnp.eye(h, dtype=w.dtype)
    return (eye[:, None, :, None] * w[:, :, None, :]).reshape(h * a, h * b)


def kernel(x, w_in, kv_norm_g, w_uk, w_uv, rel_bias, ln_v_g, ln_v_b, w_spatial, b_spatial,
           w_proj_a, w_proj_b, w_out, norm1_g, norm2_g, router_group_w, router_group_b,
           router_expert_w, router_expert_b, w_gate, w_up, w_down, final_norm_g):
    batch, seq, d = x.shape
    depth = w_in.shape[0]
    n = batch * seq
    assert depth == 1, "the final rms-norm is fused into the (single) layer's expert kernel"
    assert EXPERTS_PER_GROUP == 8
    assert seq % Q_BLK == 0 and n % min(TM_MOE, n) == 0
    assert TM_PROJ == Q_BLK == TM_MIX
    assert Q_BLK + 1 >= MAX_DISTANCE

    attn_w = N_HEADS * HEAD_DIM
    sizes = (attn_w, KV_RANK, IDX_HEADS * IDX_DIM, IDX_DIM, IDX_HEADS, GMLP_WIDTH, GMLP_WIDTH, d, d)
    offs = np.concatenate([[0], np.cumsum(sizes)])
    d0, d1 = _bias_tiles(rel_bias)
    tril = jnp.tril(jnp.ones((CHUNK, CHUNK), dtype=bool))
    row = lambda a: a.reshape(1, -1)

    x2 = x.reshape(n, d)
    l = 0
    wl = w_in[l].astype(BF16)
    wq, wc, wqi, wki, wwi, wu, wv, wga, wgb = [wl[:, offs[i]:offs[i + 1]] for i in range(9)]
    wt = jnp.concatenate([wq, wqi, wc, wwi], axis=1).T
    wuk_bd = _block_diag(jnp.transpose(w_uk[l], (1, 0, 2))).astype(BF16)
    wuv_t = jnp.transpose(w_uv[l], (1, 2, 0)).astype(BF16)

    qlat_t, qi_t, wi_t, ckv_t, ckv, ki, u, v, ga, gb = _projection(
        x2, row(norm1_g[l]), wt, wuk_bd, kv_norm_g[l].reshape(-1, 1), wc, row(kv_norm_g[l]),
        wki, wu, wv, row(ln_v_g[l]), row(ln_v_b[l]), wga, wgb)

    olat_t = _attention(qlat_t, qi_t, wi_t, ckv, ckv_t, ki, d0, d1, batch, seq)

    ws = jnp.where(tril[None], w_spatial[l], 0.0).astype(BF16)
    bs = jnp.broadcast_to(b_spatial[l][:, :, None], (GMLP_GROUPS, CHUNK, GMLP_GROUP_DIM))
    wr = jnp.concatenate(
        [router_expert_w[l], router_group_w[l],
         jnp.zeros((d, LANES - N_EXPERTS - N_GROUPS), F32)], axis=1)
    wr_hi = wr.astype(BF16)
    wr = jnp.stack([wr_hi, (wr - wr_hi.astype(F32)).astype(BF16)])
    br = jnp.concatenate(
        [router_expert_b[l], router_group_b[l],
         jnp.zeros((LANES - N_EXPERTS - N_GROUPS,), F32)]).reshape(1, LANES)
    x1, h2, comb, goh_t = _mix(
        x2, olat_t, u, v, ga, gb, wuv_t, w_proj_a[l].astype(BF16), ws, bs,
        w_proj_b[l].astype(BF16), w_out[l].astype(BF16), row(norm2_g[l]), wr, br)

    out = _moe(x1, h2, comb, goh_t, w_gate[l].astype(BF16), w_up[l].astype(BF16),
               w_down[l].astype(BF16), row(final_norm_g))
    return out.reshape(batch, seq, d)
```

```python
import functools

import numpy as np
import jax
import jax.numpy as jnp
from jax import lax
from jax.experimental import pallas as pl
from jax.experimental.pallas import tpu as pltpu

F32 = jnp.float32
BF16 = jnp.bfloat16

N_HEADS = 8
HEAD_DIM = 64
KV_RANK = 128
IDX_HEADS = 8
IDX_DIM = 64
TOPK_MAX = 256
CHUNK = 128
GMLP_GROUPS = 4
GMLP_GROUP_DIM = 128
GMLP_WIDTH = GMLP_GROUPS * GMLP_GROUP_DIM
NUM_BUCKETS = 32
MAX_DISTANCE = 128
N_GROUPS = 4
EXPERTS_PER_GROUP = 8
N_EXPERTS = N_GROUPS * EXPERTS_PER_GROUP
D_FF_EXPERT = 256
EPS = 1e-6

CKVT_ROWS = KV_RANK + 16
LANES = 128
SUBLANES = 8
VMEM_LIMIT = 52 * 1024 * 1024
NEG = -3.0e38
MASK = -1.0e30

TM_PROJ = 256
Q_BLK = 256
TM_MIX = 256
TM_MOE = 1024
MOE_ROWS = 128
MOE_PERM_ROWS = 256
MOE_EXPERTS_PER_STEP = 4
SEARCH_MAX_ITERS = 64
SEARCH_UNROLL = 2
COUNT_ACCS = 4
VERIFY_FROM = 16
VERIFY_EVERY = 4
LOG2E = 1.4426950408889634


def _dot(a, b):
    return jnp.dot(a, b, preferred_element_type=F32)


def _dot_t(a, b):
    return lax.dot_general(a, b, (((1,), (1,)), ((), ())), preferred_element_type=F32)


def _rms(x, g):
    return x * lax.rsqrt(jnp.mean(x * x, axis=-1, keepdims=True) + EPS) * g


_T_ROWS = (N_HEADS * HEAD_DIM, IDX_HEADS * IDX_DIM, KV_RANK, IDX_HEADS)
_T_OFFS = tuple(int(v) for v in np.concatenate([[0], np.cumsum(_T_ROWS)]))


def _proj_kernel(x_ref, g1_ref, wt_ref, wuk_ref, gkvc_ref, wc_ref, gkv_ref, wki_ref, wu_ref,
                 wv_ref, lng_ref, lnb_ref, wga_ref, wgb_ref,
                 qlat_ref, qi_ref, wi_ref, ckvt_ref, ckv_ref, ki_ref, u_ref, v_ref, ga_ref, gb_ref):
    hb = _rms(x_ref[...], g1_ref[...]).astype(BF16)
    t_all = _dot_t(wt_ref[...], hb)
    q_t, qi_t, c_t, w_t = [t_all[_T_OFFS[i]:_T_OFFS[i + 1]] for i in range(4)]
    qlat_t = (_dot(wuk_ref[...], q_t.astype(BF16)) * ((HEAD_DIM ** -0.5) * LOG2E)).astype(BF16)
    w_t = w_t * ((IDX_HEADS ** -0.5) * (IDX_DIM ** -0.5))
    tm = x_ref.shape[0]
    for h in range(N_HEADS):
        cols = slice(h * tm, (h + 1) * tm)
        qlat_ref[:, cols] = qlat_t[h * KV_RANK:(h + 1) * KV_RANK]
        qi_ref[:, cols] = qi_t[h * IDX_DIM:(h + 1) * IDX_DIM].astype(BF16)
        wi_ref[:, cols] = w_t[h:h + 1]
    c_n = c_t * lax.rsqrt(jnp.mean(c_t * c_t, axis=0, keepdims=True) + EPS) * gkvc_ref[...]
    ckvt_ref[...] = jnp.concatenate(
        [c_n, jnp.ones((1, tm), F32), jnp.zeros((CKVT_ROWS - KV_RANK - 1, tm), F32)],
        axis=0).astype(BF16)
    ckv_ref[...] = _rms(_dot(hb, wc_ref[...]), gkv_ref[...]).astype(BF16)
    ki_ref[...] = _dot(hb, wki_ref[...]).astype(BF16)
    u_ref[...] = jax.nn.gelu(_dot(hb, wu_ref[...])).astype(BF16)
    v = jax.nn.gelu(_dot(hb, wv_ref[...]))
    mu = jnp.mean(v, axis=-1, keepdims=True)
    var = jnp.mean(jnp.square(v - mu), axis=-1, keepdims=True)
    v_ref[...] = ((v - mu) * lax.rsqrt(var + EPS) * lng_ref[...] + lnb_ref[...]).astype(BF16)
    ga_ref[...] = jax.nn.sigmoid(_dot(hb, wga_ref[...])).astype(BF16)
    gb_ref[...] = jax.nn.sigmoid(_dot(hb, wgb_ref[...])).astype(BF16)


def _const_spec(shape):
    nd = len(shape)
    return pl.BlockSpec(shape, lambda *_: (0,) * nd)


def _projection(x2, g1, wt, wuk, gkvc, wc, gkv, wki, wu, wv, lng, lnb, wga, wgb):
    n, d = x2.shape
    tm = TM_PROJ
    row = lambda w: pl.BlockSpec((tm, w), lambda i: (i, 0))
    col = lambda h: pl.BlockSpec((h, tm), lambda i: (0, i))
    wide = lambda h: pl.BlockSpec((h, N_HEADS * tm), lambda i: (0, i))
    weights = (g1, wt, wuk, gkvc, wc, gkv, wki, wu, wv, lng, lnb, wga, wgb)
    out_shape = (
        jax.ShapeDtypeStruct((KV_RANK, N_HEADS * n), BF16),
        jax.ShapeDtypeStruct((IDX_DIM, IDX_HEADS * n), BF16),
        jax.ShapeDtypeStruct((1, IDX_HEADS * n), F32),
        jax.ShapeDtypeStruct((CKVT_ROWS, n), BF16),
        jax.ShapeDtypeStruct((n, KV_RANK), BF16),
        jax.ShapeDtypeStruct((n, IDX_DIM), BF16),
        jax.ShapeDtypeStruct((n, GMLP_WIDTH), BF16),
        jax.ShapeDtypeStruct((n, GMLP_WIDTH), BF16),
        jax.ShapeDtypeStruct((n, d), BF16),
        jax.ShapeDtypeStruct((n, d), BF16),
    )
    out_specs = (
        wide(KV_RANK), wide(IDX_DIM), wide(1), col(CKVT_ROWS),
        row(KV_RANK), row(IDX_DIM), row(GMLP_WIDTH), row(GMLP_WIDTH), row(d), row(d),
    )
    return pl.pallas_call(
        _proj_kernel,
        grid=(n // tm,),
        in_specs=[row(d)] + [_const_spec(w.shape) for w in weights],
        out_specs=out_specs,
        out_shape=out_shape,
        compiler_params=pltpu.CompilerParams(
            dimension_semantics=("arbitrary",), vmem_limit_bytes=VMEM_LIMIT),
        name="in_projection",
    )(x2, *weights)


def _attn_kernel(qlat_ref, qi_ref, wi_ref, ckv_ref, ckvt_ref, ki_ref, d0_ref, d1_ref, o_ref,
                 idx_ref, m_ref, acc_ref, tri_ref, seen_ref, *, topk, seq):
    qb = Q_BLK
    ns = qb // SUBLANES
    j_q = pl.program_id(1)
    q0 = j_q * qb
    nch = j_q + 1
    kf = float(topk)

    def chunk_rows(j):
        return pl.ds(pl.multiple_of(j * qb, qb), qb)

    krow = lax.broadcasted_iota(jnp.int32, (qb, qb), 0)
    qcol = lax.broadcasted_iota(jnp.int32, (qb, qb), 1)

    def idx_chunk(j, carry):
        rmax, rmin = carry
        zw = jnp.maximum(_dot(ki_ref[chunk_rows(j), :], qi_ref[...]), 0.0) * wi_ref[...]
        index = zw[:, 0:qb]
        for h in range(1, IDX_HEADS):
            index = index + zw[:, h * qb:(h + 1) * qb]
        valid = (j * qb + krow) <= (q0 + qcol)
        x = jnp.where(valid, index, NEG)
        idx_ref[chunk_rows(j), :] = x
        rmax = jnp.maximum(rmax, jnp.max(x, axis=0, keepdims=True))
        rmin = jnp.minimum(rmin, jnp.min(jnp.where(valid, index, -NEG), axis=0, keepdims=True))
        return rmax, rmin

    rmax, rmin = lax.fori_loop(
        0, nch, idx_chunk, (jnp.full((1, qb), NEG, F32), jnp.full((1, qb), -NEG, F32)))

    @pl.when((nch & 1) == 1)
    def _():
        idx_ref[chunk_rows(nch), :] = jnp.full((qb, qb), NEG, F32)

    def slabs(j):
        return idx_ref[chunk_rows(j), :].reshape(ns, SUBLANES, qb)

    def col_sum(c8):
        return jnp.sum(c8, axis=0, keepdims=True)

    def count_ge(thr):
        thr8 = jnp.broadcast_to(thr, (SUBLANES, qb))

        def body(jp, accs):
            pair = idx_ref[pl.ds(pl.multiple_of(jp * (2 * qb), 2 * qb), 2 * qb), :]
            accs = list(accs)
            for s in range(2 * ns):
                xs = pair[s * SUBLANES:(s + 1) * SUBLANES]
                accs[s % COUNT_ACCS] = accs[s % COUNT_ACCS] + jnp.where(xs >= thr8, 1.0, 0.0)
            return tuple(accs)

        zero = jnp.zeros((SUBLANES, qb), F32)
        accs = lax.fori_loop(0, (nch + 1) >> 1, body, (zero,) * COUNT_ACCS)
        return col_sum(functools.reduce(lambda a, b: a + b, accs))

    def bracket_extremes(lo, hi):
        lo8 = jnp.broadcast_to(lo, (SUBLANES, qb))[None]
        hi8 = jnp.broadcast_to(hi, (SUBLANES, qb))[None]

        def body(j, carry):
            bmax, bmin = carry
            xs = slabs(j)
            inb = (xs >= lo8) & (xs < hi8)
            bmax = jnp.maximum(bmax, jnp.max(jnp.where(inb, xs, NEG), axis=0))
            bmin = jnp.minimum(bmin, jnp.min(jnp.where(inb, xs, -NEG), axis=0))
            return bmax, bmin

        bmax, bmin = lax.fori_loop(
            0, nch, body,
            (jnp.full((SUBLANES, qb), NEG, F32), jnp.full((SUBLANES, qb), -NEG, F32)))
        return jnp.max(bmax, axis=0, keepdims=True), jnp.min(bmin, axis=0, keepdims=True)

    def resolve_small_brackets(lo, hi, c_lo, c_hi, tie):
        bmax, bmin = bracket_extremes(lo, hi)
        is_open = c_lo > kf
        top_only = is_open & (bmax != bmin) & ((kf - c_hi) == 1.0)
        lo = jnp.where(top_only, bmax, lo)
        tie = jnp.where(is_open & ((bmax == bmin) | top_only), 1.0, tie)
        return lo, tie

    nvalid = (q0 + 1 + lax.broadcasted_iota(jnp.int32, (1, qb), 1)).astype(F32)
    hi0 = rmax + (jnp.abs(rmax) * 1e-6 + 1e-30)

    def open_rows(c_lo, tie):
        return jnp.max(jnp.where((c_lo > kf) & (tie < 0.5), 1.0, 0.0)) > 0.0

    def search_cond(state):
        it, _, _, c_lo, _, tie = state
        return (it < SEARCH_MAX_ITERS) & open_rows(c_lo, tie)

    def search_body(state):
        it, lo, hi, c_lo, c_hi, tie = state
        for _ in range(SEARCH_UNROLL):
            mid = 0.5 * (lo + hi)
            c = count_ge(mid)
            ok = c >= kf
            lo, hi = jnp.where(ok, mid, lo), jnp.where(ok, hi, mid)
            c_lo, c_hi = jnp.where(ok, c, c_lo), jnp.where(ok, c_hi, c)
        it = it + SEARCH_UNROLL
        check = (it >= VERIFY_FROM) & (((it - VERIFY_FROM) & (VERIFY_EVERY - 1)) == 0)
        lo, tie = lax.cond(check, lambda: resolve_small_brackets(lo, hi, c_lo, c_hi, tie),
                           lambda: (lo, tie))
        return it, lo, hi, c_lo, c_hi, tie

    _, lo, hi, c_lo, c_hi, _ = lax.while_loop(
        search_cond, search_body,
        (jnp.int32(0), rmin, hi0, nvalid, jnp.zeros((1, qb), F32), jnp.zeros((1, qb), F32)))

    need = jnp.where(c_lo > kf, kf - c_hi, float(seq + 1))
    tri_ref[...] = jnp.where(lax.broadcasted_iota(jnp.int32, (qb, qb), 1)
                             <= lax.broadcasted_iota(jnp.int32, (qb, qb), 0), 1.0, 0.0).astype(BF16)
    seen_ref[...] = jnp.zeros(seen_ref.shape, F32)

    m_ref[...] = jnp.full(m_ref.shape, MASK, F32)
    acc_ref[...] = jnp.zeros(acc_ref.shape, F32)

    def att_step(j, bias_ref):
        ckc = ckv_ref[chunk_rows(j), :]
        ckc_t = ckvt_ref[:, chunk_rows(j)]
        xs = idx_ref[chunk_rows(j), :]
        in_bracket = jnp.where(xs >= lo, jnp.where(xs < hi, 1.0, 0.0), 0.0)
        rank = seen_ref[...] + _dot(tri_ref[...], in_bracket.astype(BF16))
        seen_ref[...] = rank[qb - 1:qb, :]
        madd = jnp.where(xs >= hi, 0.0,
                         jnp.where(xs >= lo, jnp.where(rank <= need, 0.0, MASK), MASK))
        s_all = _dot(ckc, qlat_ref[...])
        s_h = []
        for h in range(N_HEADS):
            s = s_all[:, h * qb:(h + 1) * qb] + madd
            if bias_ref is not None:
                s = s + bias_ref[:, h * qb:(h + 1) * qb]
            s_h.append(s)
        s_all = jnp.concatenate(s_h, axis=1)
        m_old = m_ref[...]
        m_new = jnp.maximum(m_old, jnp.max(s_all, axis=0, keepdims=True))
        alpha = jnp.exp2(m_old - m_new)
        p = jnp.exp2((s_all - m_new).astype(BF16))
        acc_ref[...] = alpha * acc_ref[...] + _dot(ckc_t, p)
        m_ref[...] = m_new

    def far_pair(jp, carry):
        att_step(2 * jp, None)
        att_step(2 * jp + 1, None)
        return carry

    nfar = jnp.maximum(nch - 2, 0)
    lax.fori_loop(0, nfar >> 1, far_pair, 0)

    @pl.when((nfar & 1) == 1)
    def _():
        att_step(nfar - 1, None)

    @pl.when(nch >= 2)
    def _():
        att_step(nch - 2, d1_ref)

    att_step(nch - 1, d0_ref)

    o_ref[...] = (acc_ref[0:KV_RANK, :] / acc_ref[KV_RANK:KV_RANK + 1, :]).astype(BF16)


def _attention(qlat_w, qi_w, wi_w, ckv, ckv_t, ki, d0, d1, batch, seq):
    qb = Q_BLK
    nq = seq // qb
    n = batch * seq
    topk = min(TOPK_MAX, seq // 4)
    kern = functools.partial(_attn_kernel, topk=topk, seq=seq)
    wide = lambda h: pl.BlockSpec((h, N_HEADS * qb), lambda b, j: (0, b * nq + j))
    return pl.pallas_call(
        kern,
        grid=(batch, nq),
        in_specs=[
            wide(KV_RANK), wide(IDX_DIM), wide(1),
            pl.BlockSpec((seq, KV_RANK), lambda b, j: (b, 0)),
            pl.BlockSpec((CKVT_ROWS, seq), lambda b, j: (0, b)),
            pl.BlockSpec((seq, IDX_DIM), lambda b, j: (b, 0)),
            _const_spec(d0.shape), _const_spec(d1.shape),
        ],
        out_specs=wide(KV_RANK),
        out_shape=jax.ShapeDtypeStruct((KV_RANK, N_HEADS * n), BF16),
        scratch_shapes=[
            pltpu.VMEM((seq + qb, qb), F32),
            pltpu.VMEM((1, N_HEADS * qb), F32),
            pltpu.VMEM((CKVT_ROWS, N_HEADS * qb), F32),
            pltpu.VMEM((qb, qb), BF16),
            pltpu.VMEM((1, qb), F32),
        ],
        compiler_params=pltpu.CompilerParams(
            dimension_semantics=("arbitrary", "arbitrary"), vmem_limit_bytes=VMEM_LIMIT),
        name="sparse_attention",
    )(qlat_w, qi_w, wi_w, ckv, ckv_t, ki, d0, d1)


def _mix_kernel(x_ref, ol_ref, u_ref, v_ref, ga_ref, gb_ref, wuv_ref, wpa_ref, ws_ref, bs_ref,
                wpb_ref, wout_ref, g2_ref, wr_ref, br_ref, x1_ref, h2_ref, comb_ref, goh_ref):
    tm = x_ref.shape[0]
    o_at = jnp.concatenate(
        [_dot(wuv_ref[h], ol_ref[:, h * tm:(h + 1) * tm]) for h in range(N_HEADS)], axis=0)
    y_a = _dot(jnp.transpose(o_at).astype(BF16), wpa_ref[...])
    v = v_ref[...]
    u = u_ref[...].astype(F32)
    rows = []
    for c in range(tm // CHUNK):
        cols = []
        for g in range(GMLP_GROUPS):
            vc = v[c * CHUNK:(c + 1) * CHUNK, g * GMLP_GROUP_DIM:(g + 1) * GMLP_GROUP_DIM]
            cols.append(_dot(ws_ref[g], vc) + bs_ref[g])
        rows.append(jnp.concatenate(cols, axis=1))
    s = jnp.concatenate(rows, axis=0)
    y_b = _dot((u * s).astype(BF16), wpb_ref[...])
    merged = ga_ref[...].astype(F32) * y_a + gb_ref[...].astype(F32) * y_b
    x1 = x_ref[...] + _dot(merged.astype(BF16), wout_ref[...])
    x1_ref[...] = x1
    h2 = _rms(x1, g2_ref[...])
    h2_hi = h2.astype(BF16)
    h2_ref[...] = h2_hi

    h2_lo = (h2 - h2_hi.astype(F32)).astype(BF16)
    logits = (_dot(h2_hi, wr_ref[0]) + (_dot(h2_lo, wr_ref[0]) + _dot(h2_hi, wr_ref[1]))
              + br_ref[...])
    lane = lax.broadcasted_iota(jnp.int32, logits.shape, 1)
    big = jnp.int32(4 * LANES)
    is_g = (lane >= N_EXPERTS) & (lane < N_EXPERTS + N_GROUPS)
    lg = jnp.where(is_g, logits, NEG)
    gmax = jnp.max(lg, axis=1, keepdims=True)
    g_w = 1.0 / jnp.sum(jnp.where(is_g, jnp.exp(lg - gmax), 0.0), axis=1, keepdims=True)
    g_idx = jnp.min(jnp.where(is_g & (lg == gmax), lane, big), axis=1, keepdims=True) - N_EXPERTS
    in_g = (lane < N_EXPERTS) & ((lane >> 3) == g_idx)
    le = jnp.where(in_g, logits, NEG)
    emax = jnp.max(le, axis=1, keepdims=True)
    ee = jnp.where(in_g, jnp.exp(le - emax), 0.0)
    ep = ee / jnp.sum(ee, axis=1, keepdims=True)
    ep1 = jnp.where(in_g, ep, -1.0)
    m1 = jnp.max(ep1, axis=1, keepdims=True)
    i1 = jnp.min(jnp.where(ep1 == m1, lane, big), axis=1, keepdims=True)
    ep2 = jnp.where(lane == i1, -1.0, ep1)
    m2 = jnp.max(ep2, axis=1, keepdims=True)
    i2 = jnp.min(jnp.where(ep2 == m2, lane, big), axis=1, keepdims=True)
    tot = m1 + m2
    comb_ref[...] = (jnp.where(lane == i1, g_w * m1 / tot, 0.0)
                     + jnp.where(lane == i2, g_w * m2 / tot, 0.0))
    goh_ref[...] = jnp.transpose(jnp.where(lane == g_idx, 1.0, 0.0))[0:SUBLANES, :]


def _mix(x2, olat_t, u, v, ga, gb, wuv, wpa, ws, bs, wpb, wout, g2, wr, br):
    n, d = x2.shape
    tm = TM_MIX
    row = lambda w: pl.BlockSpec((tm, w), lambda i: (i, 0))
    weights = (wuv, wpa, ws, bs, wpb, wout, g2, wr, br)
    return pl.pallas_call(
        _mix_kernel,
        grid=(n // tm,),
        in_specs=[row(d), pl.BlockSpec((KV_RANK, N_HEADS * tm), lambda i: (0, i)),
                  row(GMLP_WIDTH), row(GMLP_WIDTH), row(d), row(d)]
        + [_const_spec(w.shape) for w in weights],
        out_specs=(row(d), row(d), row(LANES), pl.BlockSpec((SUBLANES, tm), lambda i: (0, i))),
        out_shape=(jax.ShapeDtypeStruct((n, d), F32), jax.ShapeDtypeStruct((n, d), BF16),
                   jax.ShapeDtypeStruct((n, LANES), F32),
                   jax.ShapeDtypeStruct((SUBLANES, n), F32)),
        compiler_params=pltpu.CompilerParams(
            dimension_semantics=("arbitrary",), vmem_limit_bytes=VMEM_LIMIT),
        name="merge_router",
    )(x2, olat_t, u, v, ga, gb, *weights)


def _moe_kernel(x1_ref, h2_ref, comb_ref, goh_ref, tri_ref, wg_ref, wu_ref, wd_ref, gf_ref, o_ref,
                perm_ref, pos_ref, hs_ref, cs_ref, ys_ref, seg_ref):
    step = pl.program_id(1)
    tt, d = hs_ref.shape
    rb = MOE_ROWS
    pc = MOE_PERM_ROWS

    @pl.when(step == 0)
    def _():
        goh = goh_ref[...]
        before = _dot(goh.astype(BF16), tri_ref[...])
        pos = jnp.zeros((1, tt), F32)
        start = jnp.float32(0.0)
        for g in range(N_GROUPS):
            cnt = jnp.sum(goh[g:g + 1, :])
            seg_ref[g] = jnp.floor(start * (1.0 / rb)).astype(jnp.int32)
            seg_ref[N_GROUPS + g] = jnp.ceil((start + cnt) * (1.0 / rb)).astype(jnp.int32)
            pos = pos + goh[g:g + 1, :] * (start + before[g:g + 1, :])
            start = start + cnt
        pos_ref[...] = pos
        for r in range(tt // pc):
            dst = (lax.broadcasted_iota(jnp.int32, (pc, tt), 0) + r * pc).astype(F32)
            perm_ref[r * pc:(r + 1) * pc, :] = jnp.where(pos == dst, 1.0, 0.0).astype(BF16)
        perm = perm_ref[...]
        hs_ref[...] = _dot(perm, h2_ref[...]).astype(BF16)
        comb = comb_ref[...]
        comb_hi = comb.astype(BF16)
        comb_lo = (comb - comb_hi.astype(F32)).astype(BF16)
        cs_ref[...] = _dot(perm, comb_hi) + _dot(perm, comb_lo)
        ys_ref[...] = jnp.zeros(ys_ref.shape, F32)

    group = step // (EXPERTS_PER_GROUP // MOE_EXPERTS_PER_STEP)
    first = seg_ref[group]
    last = seg_ref[N_GROUPS + group]

    def block(b, carry):
        rows = pl.ds(pl.multiple_of(b * rb, rb), rb)
        hb = hs_ref[rows, :]
        cb = cs_ref[rows, :]
        lane = lax.broadcasted_iota(jnp.int32, cb.shape, 1)
        y = jnp.zeros((rb, d), F32)
        for k in range(MOE_EXPERTS_PER_STEP):
            e = step * MOE_EXPERTS_PER_STEP + k
            w_e = jnp.sum(jnp.where(lane == e, cb, 0.0), axis=1, keepdims=True)
            act = jax.nn.silu(_dot(hb, wg_ref[k])) * _dot(hb, wu_ref[k]) * w_e
            y = y + _dot(act.astype(BF16), wd_ref[k])
        ys_ref[rows, :] += y
        return carry

    lax.fori_loop(first, last, block, 0)

    @pl.when(step == pl.num_programs(1) - 1)
    def _():
        ys = ys_ref[...].astype(BF16)
        pos_col = jnp.transpose(jnp.broadcast_to(pos_ref[...], (SUBLANES, tt)))[:, 0:1]
        for r in range(tt // pc):
            rows = slice(r * pc, (r + 1) * pc)
            src = lax.broadcasted_iota(jnp.int32, (pc, tt), 1).astype(F32)
            unperm = jnp.where(pos_col[rows] == src, 1.0, 0.0).astype(BF16)
            o_ref[rows, :] = _rms(x1_ref[rows, :] + _dot(unperm, ys), gf_ref[...])


def _moe(x1, h2, comb, goh_t, wg, wu, wd, gf):
    n, d = x1.shape
    tt = min(TM_MOE, n)
    eps = MOE_EXPERTS_PER_STEP
    assert EXPERTS_PER_GROUP % eps == 0 and tt % MOE_ROWS == 0 and tt % MOE_PERM_ROWS == 0
    row = lambda w: pl.BlockSpec((tt, w), lambda i, s: (i, 0))
    tri = (jnp.arange(tt)[:, None] < jnp.arange(tt)[None, :]).astype(BF16)
    return pl.pallas_call(
        _moe_kernel,
        grid=(n // tt, N_EXPERTS // eps),
        in_specs=[row(d), row(d), row(LANES),
                  pl.BlockSpec((SUBLANES, tt), lambda i, s: (0, i)),
                  _const_spec(tri.shape),
                  pl.BlockSpec((eps, d, D_FF_EXPERT), lambda i, s: (s, 0, 0)),
                  pl.BlockSpec((eps, d, D_FF_EXPERT), lambda i, s: (s, 0, 0)),
                  pl.BlockSpec((eps, D_FF_EXPERT, d), lambda i, s: (s, 0, 0)),
                  _const_spec(gf.shape)],
        out_specs=row(d),
        out_shape=jax.ShapeDtypeStruct((n, d), F32),
        scratch_shapes=[
            pltpu.VMEM((tt, tt), BF16),
            pltpu.VMEM((1, tt), F32),
            pltpu.VMEM((tt, d), BF16),
            pltpu.VMEM((tt, LANES), F32),
            pltpu.VMEM((tt, d), F32),
            pltpu.SMEM((2 * N_GROUPS,), jnp.int32),
        ],
        compiler_params=pltpu.CompilerParams(
            dimension_semantics=("arbitrary", "arbitrary"), vmem_limit_bytes=VMEM_LIMIT),
        name="experts",
    )(x1, h2, comb, goh_t, tri, wg, wu, wd, gf)


def _t5_bucket(n):
    max_exact = NUM_BUCKETS // 2
    nf = jnp.maximum(n, 1).astype(F32)
    large = max_exact + jnp.floor(
        jnp.log(nf / max_exact) / np.float32(np.log(MAX_DISTANCE / max_exact))
        * (NUM_BUCKETS - max_exact)).astype(jnp.int32)
    large = jnp.minimum(large, NUM_BUCKETS - 1)
    return jnp.where(n < max_exact, n, large)


def _bias_tiles(rel_bias):
    qb = Q_BLK
    k = jnp.arange(qb, dtype=jnp.int32)[:, None]
    q = jnp.arange(qb, dtype=jnp.int32)[None, :]
    rel = (rel_bias - rel_bias[NUM_BUCKETS - 1]) * LOG2E

    def tile(dist):
        onehot = (_t5_bucket(dist)[:, :, None] == jnp.arange(NUM_BUCKETS)).astype(F32)
        t = jnp.einsum("kqb,bh->khq", onehot, rel, precision=lax.Precision.HIGHEST)
        return t.reshape(qb, N_HEADS * qb)

    return tile(jnp.maximum(q - k, 0)), tile(qb + q - k)


def _block_diag(w):
    h, a, b = w.shape
    eye = jnp.eye(h, dtype=w.dtype)
    return (eye[:, None, :, None] * w[:, :, None, :]).reshape(h * a, h * b)


def kernel(x, w_in, kv_norm_g, w_uk, w_uv, rel_bias, ln_v_g, ln_v_b, w_spatial, b_spatial,
           w_proj_a, w_proj_b, w_out, norm1_g, norm2_g, router_group_w, router_group_b,
           router_expert_w, router_expert_b, w_gate, w_up, w_down, final_norm_g):
    batch, seq, d = x.shape
    depth = w_in.shape[0]
    n = batch * seq
    assert depth == 1, "the final rms-norm is fused into the (single) layer's expert kernel"
    assert EXPERTS_PER_GROUP == 8
    assert seq % Q_BLK == 0 and n % min(TM_MOE, n) == 0
    assert TM_PROJ == Q_BLK == TM_MIX
    assert Q_BLK + 1 >= MAX_DISTANCE

    attn_w = N_HEADS * HEAD_DIM
    sizes = (attn_w, KV_RANK, IDX_HEADS * IDX_DIM, IDX_DIM, IDX_HEADS, GMLP_WIDTH, GMLP_WIDTH, d, d)
    offs = np.concatenate([[0], np.cumsum(sizes)])
    d0, d1 = _bias_tiles(rel_bias)
    tril = jnp.tril(jnp.ones((CHUNK, CHUNK), dtype=bool))
    row = lambda a: a.reshape(1, -1)

    x2 = x.reshape(n, d)
    l = 0
    wl = w_in[l].astype(BF16)
    wq, wc, wqi, wki, wwi, wu, wv, wga, wgb = [wl[:, offs[i]:offs[i + 1]] for i in range(9)]
    wt = jnp.concatenate([wq, wqi, wc, wwi], axis=1).T
    wuk_bd = _block_diag(jnp.transpose(w_uk[l], (1, 0, 2))).astype(BF16)
    wuv_t = jnp.transpose(w_uv[l], (1, 2, 0)).astype(BF16)

    qlat_t, qi_t, wi_t, ckv_t, ckv, ki, u, v, ga, gb = _projection(
        x2, row(norm1_g[l]), wt, wuk_bd, kv_norm_g[l].reshape(-1, 1), wc, row(kv_norm_g[l]),
        wki, wu, wv, row(ln_v_g[l]), row(ln_v_b[l]), wga, wgb)

    olat_t = _attention(qlat_t, qi_t, wi_t, ckv, ckv_t, ki, d0, d1, batch, seq)

    ws = jnp.where(tril[None], w_spatial[l], 0.0).astype(BF16)
    bs = jnp.broadcast_to(b_spatial[l][:, :, None], (GMLP_GROUPS, CHUNK, GMLP_GROUP_DIM))
    wr = jnp.concatenate(
        [router_expert_w[l], router_group_w[l],
         jnp.zeros((d, LANES - N_EXPERTS - N_GROUPS), F32)], axis=1)
    wr_hi = wr.astype(BF16)
    wr = jnp.stack([wr_hi, (wr - wr_hi.astype(F32)).astype(BF16)])
    br = jnp.concatenate(
        [router_expert_b[l], router_group_b[l],
         jnp.zeros((LANES - N_EXPERTS - N_GROUPS,), F32)]).reshape(1, LANES)
    x1, h2, comb, goh_t = _mix(
        x2, olat_t, u, v, ga, gb, wuv_t, w_proj_a[l].astype(BF16), ws, bs,
        w_proj_b[l].astype(BF16), w_out[l].astype(BF16), row(norm2_g[l]), wr, br)

    out = _moe(x1, h2, comb, goh_t, w_gate[l].astype(BF16), w_up[l].astype(BF16),
               w_down[l].astype(BF16), row(final_norm_g))
    return out.reshape(batch, seq, d)
```

```python
import functools

import numpy as np
import jax
import jax.numpy as jnp
from jax import lax
from jax.experimental import pallas as pl
from jax.experimental.pallas import tpu as pltpu

F32 = jnp.float32
BF16 = jnp.bfloat16

N_HEADS = 8
HEAD_DIM = 64
KV_RANK = 128
IDX_HEADS = 8
IDX_DIM = 64
TOPK_MAX = 256
CHUNK = 128
GMLP_GROUPS = 4
GMLP_GROUP_DIM = 128
GMLP_WIDTH = GMLP_GROUPS * GMLP_GROUP_DIM
NUM_BUCKETS = 32
MAX_DISTANCE = 128
N_GROUPS = 4
EXPERTS_PER_GROUP = 8
N_EXPERTS = N_GROUPS * EXPERTS_PER_GROUP
D_FF_EXPERT = 256
EPS = 1e-6

CKVT_ROWS = KV_RANK + 16
LANES = 128
SUBLANES = 8
VMEM_LIMIT = 52 * 1024 * 1024
NEG = -3.0e38
MASK = -1.0e30

TM_PROJ = 512
Q_BLK = 256
TM_MIX = 512
TM_MOE = 1024
MOE_ROWS = 128
MOE_PERM_ROWS = 256
MOE_EXPERTS_PER_STEP = 4
SEARCH_MAX_ITERS = 64
SEARCH_UNROLL = 2
COUNT_ACCS = 4
VERIFY_FROM = 16
VERIFY_EVERY = 4
LOG2E = 1.4426950408889634


def _dot(a, b):
    return jnp.dot(a, b, preferred_element_type=F32)


def _dot_t(a, b):
    return lax.dot_general(a, b, (((1,), (1,)), ((), ())), preferred_element_type=F32)


def _rms(x, g):
    return x * lax.rsqrt(jnp.mean(x * x, axis=-1, keepdims=True) + EPS) * g


_T_ROWS = (N_HEADS * HEAD_DIM, IDX_HEADS * IDX_DIM, KV_RANK, IDX_HEADS)
_T_OFFS = tuple(int(v) for v in np.concatenate([[0], np.cumsum(_T_ROWS)]))


def _proj_kernel(x_ref, g1_ref, wt_ref, wuk_ref, gkvc_ref, wc_ref, gkv_ref, wki_ref, wu_ref,
                 wv_ref, lng_ref, lnb_ref, wga_ref, wgb_ref,
                 qlat_ref, qi_ref, wi_ref, ckvt_ref, ckv_ref, ki_ref, u_ref, v_ref, ga_ref, gb_ref):
    hb = _rms(x_ref[...], g1_ref[...]).astype(BF16)
    t_all = _dot_t(wt_ref[...], hb)
    q_t, qi_t, c_t, w_t = [t_all[_T_OFFS[i]:_T_OFFS[i + 1]] for i in range(4)]
    qlat_t = (_dot(wuk_ref[...], q_t.astype(BF16)) * ((HEAD_DIM ** -0.5) * LOG2E)).astype(BF16)
    w_t = w_t * ((IDX_HEADS ** -0.5) * (IDX_DIM ** -0.5))
    tm = x_ref.shape[0]
    qb = Q_BLK
    for blk in range(tm // qb):
        toks = slice(blk * qb, (blk + 1) * qb)
        for h in range(N_HEADS):
            cols = slice((blk * N_HEADS + h) * qb, (blk * N_HEADS + h + 1) * qb)
            qlat_ref[:, cols] = qlat_t[h * KV_RANK:(h + 1) * KV_RANK, toks]
            qi_ref[:, cols] = qi_t[h * IDX_DIM:(h + 1) * IDX_DIM, toks].astype(BF16)
            wi_ref[:, cols] = w_t[h:h + 1, toks]
    c_n = c_t * lax.rsqrt(jnp.mean(c_t * c_t, axis=0, keepdims=True) + EPS) * gkvc_ref[...]
    ckvt_ref[...] = jnp.concatenate(
        [c_n, jnp.ones((1, tm), F32), jnp.zeros((CKVT_ROWS - KV_RANK - 1, tm), F32)],
        axis=0).astype(BF16)
    ckv_ref[...] = _rms(_dot(hb, wc_ref[...]), gkv_ref[...]).astype(BF16)
    ki_ref[...] = _dot(hb, wki_ref[...]).astype(BF16)
    u_ref[...] = jax.nn.gelu(_dot(hb, wu_ref[...])).astype(BF16)
    v = jax.nn.gelu(_dot(hb, wv_ref[...]))
    mu = jnp.mean(v, axis=-1, keepdims=True)
    var = jnp.mean(jnp.square(v - mu), axis=-1, keepdims=True)
    v_ref[...] = ((v - mu) * lax.rsqrt(var + EPS) * lng_ref[...] + lnb_ref[...]).astype(BF16)
    ga_ref[...] = jax.nn.sigmoid(_dot(hb, wga_ref[...])).astype(BF16)
    gb_ref[...] = jax.nn.sigmoid(_dot(hb, wgb_ref[...])).astype(BF16)


def _const_spec(shape):
    nd = len(shape)
    return pl.BlockSpec(shape, lambda *_: (0,) * nd)


def _projection(x2, g1, wt, wuk, gkvc, wc, gkv, wki, wu, wv, lng, lnb, wga, wgb):
    n, d = x2.shape
    tm = TM_PROJ
    row = lambda w: pl.BlockSpec((tm, w), lambda i: (i, 0))
    col = lambda h: pl.BlockSpec((h, tm), lambda i: (0, i))
    wide = lambda h: pl.BlockSpec((h, N_HEADS * tm), lambda i: (0, i))
    weights = (g1, wt, wuk, gkvc, wc, gkv, wki, wu, wv, lng, lnb, wga, wgb)
    out_shape = (
        jax.ShapeDtypeStruct((KV_RANK, N_HEADS * n), BF16),
        jax.ShapeDtypeStruct((IDX_DIM, IDX_HEADS * n), BF16),
        jax.ShapeDtypeStruct((1, IDX_HEADS * n), F32),
        jax.ShapeDtypeStruct((CKVT_ROWS, n), BF16),
        jax.ShapeDtypeStruct((n, KV_RANK), BF16),
        jax.ShapeDtypeStruct((n, IDX_DIM), BF16),
        jax.ShapeDtypeStruct((n, GMLP_WIDTH), BF16),
        jax.ShapeDtypeStruct((n, GMLP_WIDTH), BF16),
        jax.ShapeDtypeStruct((n, d), BF16),
        jax.ShapeDtypeStruct((n, d), BF16),
    )
    out_specs = (
        wide(KV_RANK), wide(IDX_DIM), wide(1), col(CKVT_ROWS),
        row(KV_RANK), row(IDX_DIM), row(GMLP_WIDTH), row(GMLP_WIDTH), row(d), row(d),
    )
    return pl.pallas_call(
        _proj_kernel,
        grid=(n // tm,),
        in_specs=[row(d)] + [_const_spec(w.shape) for w in weights],
        out_specs=out_specs,
        out_shape=out_shape,
        compiler_params=pltpu.CompilerParams(
            dimension_semantics=("arbitrary",), vmem_limit_bytes=VMEM_LIMIT),
        name="in_projection",
    )(x2, *weights)


def _attn_kernel(qlat_ref, qi_ref, wi_ref, ckv_ref, ckvt_ref, ki_ref, d0_ref, d1_ref, o_ref,
                 idx_ref, m_ref, acc_ref, tri_ref, seen_ref, *, topk, seq):
    qb = Q_BLK
    ns = qb // SUBLANES
    j_q = pl.program_id(1)
    q0 = j_q * qb
    nch = j_q + 1
    kf = float(topk)

    def chunk_rows(j):
        return pl.ds(pl.multiple_of(j * qb, qb), qb)

    krow = lax.broadcasted_iota(jnp.int32, (qb, qb), 0)
    qcol = lax.broadcasted_iota(jnp.int32, (qb, qb), 1)

    def idx_chunk(j, carry):
        rmax, rmin = carry
        zw = jnp.maximum(_dot(ki_ref[chunk_rows(j), :], qi_ref[...]), 0.0) * wi_ref[...]
        index = zw[:, 0:qb]
        for h in range(1, IDX_HEADS):
            index = index + zw[:, h * qb:(h + 1) * qb]
        valid = (j * qb + krow) <= (q0 + qcol)
        x = jnp.where(valid, index, NEG)
        idx_ref[chunk_rows(j), :] = x
        rmax = jnp.maximum(rmax, jnp.max(x, axis=0, keepdims=True))
        rmin = jnp.minimum(rmin, jnp.min(jnp.where(valid, index, -NEG), axis=0, keepdims=True))
        return rmax, rmin

    rmax, rmin = lax.fori_loop(
        0, nch, idx_chunk, (jnp.full((1, qb), NEG, F32), jnp.full((1, qb), -NEG, F32)))

    @pl.when((nch & 1) == 1)
    def _():
        idx_ref[chunk_rows(nch), :] = jnp.full((qb, qb), NEG, F32)

    def slabs(j):
        return idx_ref[chunk_rows(j), :].reshape(ns, SUBLANES, qb)

    def col_sum(c8):
        return jnp.sum(c8, axis=0, keepdims=True)

    def count_ge(thr):
        thr8 = jnp.broadcast_to(thr, (SUBLANES, qb))

        def body(jp, accs):
            pair = idx_ref[pl.ds(pl.multiple_of(jp * (2 * qb), 2 * qb), 2 * qb), :]
            accs = list(accs)
            for s in range(2 * ns):
                xs = pair[s * SUBLANES:(s + 1) * SUBLANES]
                accs[s % COUNT_ACCS] = accs[s % COUNT_ACCS] + jnp.where(xs >= thr8, 1.0, 0.0)
            return tuple(accs)

        zero = jnp.zeros((SUBLANES, qb), F32)
        accs = lax.fori_loop(0, (nch + 1) >> 1, body, (zero,) * COUNT_ACCS)
        return col_sum(functools.reduce(lambda a, b: a + b, accs))

    def bracket_extremes(lo, hi):
        lo8 = jnp.broadcast_to(lo, (SUBLANES, qb))[None]
        hi8 = jnp.broadcast_to(hi, (SUBLANES, qb))[None]

        def body(j, carry):
            bmax, bmin = carry
            xs = slabs(j)
            inb = (xs >= lo8) & (xs < hi8)
            bmax = jnp.maximum(bmax, jnp.max(jnp.where(inb, xs, NEG), axis=0))
            bmin = jnp.minimum(bmin, jnp.min(jnp.where(inb, xs, -NEG), axis=0))
            return bmax, bmin

        bmax, bmin = lax.fori_loop(
            0, nch, body,
            (jnp.full((SUBLANES, qb), NEG, F32), jnp.full((SUBLANES, qb), -NEG, F32)))
        return jnp.max(bmax, axis=0, keepdims=True), jnp.min(bmin, axis=0, keepdims=True)

    def resolve_small_brackets(lo, hi, c_lo, c_hi, tie):
        bmax, bmin = bracket_extremes(lo, hi)
        is_open = c_lo > kf
        top_only = is_open & (bmax != bmin) & ((kf - c_hi) == 1.0)
        lo = jnp.where(top_only, bmax, lo)
        tie = jnp.where(is_open & ((bmax == bmin) | top_only), 1.0, tie)
        return lo, tie

    nvalid = (q0 + 1 + lax.broadcasted_iota(jnp.int32, (1, qb), 1)).astype(F32)
    hi0 = rmax + (jnp.abs(rmax) * 1e-6 + 1e-30)

    def open_rows(c_lo, tie):
        return jnp.max(jnp.where((c_lo > kf) & (tie < 0.5), 1.0, 0.0)) > 0.0

    def search_cond(state):
        it, _, _, c_lo, _, tie = state
        return (it < SEARCH_MAX_ITERS) & open_rows(c_lo, tie)

    def search_body(state):
        it, lo, hi, c_lo, c_hi, tie = state
        for _ in range(SEARCH_UNROLL):
            mid = 0.5 * (lo + hi)
            c = count_ge(mid)
            ok = c >= kf
            lo, hi = jnp.where(ok, mid, lo), jnp.where(ok, hi, mid)
            c_lo, c_hi = jnp.where(ok, c, c_lo), jnp.where(ok, c_hi, c)
        it = it + SEARCH_UNROLL
        check = (it >= VERIFY_FROM) & (((it - VERIFY_FROM) & (VERIFY_EVERY - 1)) == 0)
        lo, tie = lax.cond(check, lambda: resolve_small_brackets(lo, hi, c_lo, c_hi, tie),
                           lambda: (lo, tie))
        return it, lo, hi, c_lo, c_hi, tie

    _, lo, hi, c_lo, c_hi, _ = lax.while_loop(
        search_cond, search_body,
        (jnp.int32(0), rmin, hi0, nvalid, jnp.zeros((1, qb), F32), jnp.zeros((1, qb), F32)))

    need = jnp.where(c_lo > kf, kf - c_hi, float(seq + 1))
    tri_ref[...] = jnp.where(lax.broadcasted_iota(jnp.int32, (qb, qb), 1)
                             <= lax.broadcasted_iota(jnp.int32, (qb, qb), 0), 1.0, 0.0).astype(BF16)
    seen_ref[...] = jnp.zeros(seen_ref.shape, F32)

    m_ref[...] = jnp.full(m_ref.shape, MASK, F32)
    acc_ref[...] = jnp.zeros(acc_ref.shape, F32)

    def att_step(j, bias_ref):
        ckc = ckv_ref[chunk_rows(j), :]
        ckc_t = ckvt_ref[:, chunk_rows(j)]
        xs = idx_ref[chunk_rows(j), :]
        in_bracket = jnp.where(xs >= lo, jnp.where(xs < hi, 1.0, 0.0), 0.0)
        rank = seen_ref[...] + _dot(tri_ref[...], in_bracket.astype(BF16))
        seen_ref[...] = rank[qb - 1:qb, :]
        madd = jnp.where(xs >= hi, 0.0,
                         jnp.where(xs >= lo, jnp.where(rank <= need, 0.0, MASK), MASK))
        s_all = _dot(ckc, qlat_ref[...])
        s_h = []
        for h in range(N_HEADS):
            s = s_all[:, h * qb:(h + 1) * qb] + madd
            if bias_ref is not None:
                s = s + bias_ref[:, h * qb:(h + 1) * qb]
            s_h.append(s)
        s_all = jnp.concatenate(s_h, axis=1)
        m_old = m_ref[...]
        m_new = jnp.maximum(m_old, jnp.max(s_all, axis=0, keepdims=True))
        alpha = jnp.exp2(m_old - m_new)
        p = jnp.exp2((s_all - m_new).astype(BF16))
        acc_ref[...] = alpha * acc_ref[...] + _dot(ckc_t, p)
        m_ref[...] = m_new

    def far_pair(jp, carry):
        att_step(2 * jp, None)
        att_step(2 * jp + 1, None)
        return carry

    nfar = jnp.maximum(nch - 2, 0)
    lax.fori_loop(0, nfar >> 1, far_pair, 0)

    @pl.when((nfar & 1) == 1)
    def _():
        att_step(nfar - 1, None)

    @pl.when(nch >= 2)
    def _():
        att_step(nch - 2, d1_ref)

    att_step(nch - 1, d0_ref)

    o_ref[...] = (acc_ref[0:KV_RANK, :] / acc_ref[KV_RANK:KV_RANK + 1, :]).astype(BF16)


def _attention(qlat_w, qi_w, wi_w, ckv, ckv_t, ki, d0, d1, batch, seq):
    qb = Q_BLK
    nq = seq // qb
    n = batch * seq
    topk = min(TOPK_MAX, seq // 4)
    kern = functools.partial(_attn_kernel, topk=topk, seq=seq)
    wide = lambda h: pl.BlockSpec((h, N_HEADS * qb), lambda b, j: (0, b * nq + j))
    return pl.pallas_call(
        kern,
        grid=(batch, nq),
        in_specs=[
            wide(KV_RANK), wide(IDX_DIM), wide(1),
            pl.BlockSpec((seq, KV_RANK), lambda b, j: (b, 0)),
            pl.BlockSpec((CKVT_ROWS, seq), lambda b, j: (0, b)),
            pl.BlockSpec((seq, IDX_DIM), lambda b, j: (b, 0)),
            _const_spec(d0.shape), _const_spec(d1.shape),
        ],
        out_specs=wide(KV_RANK),
        out_shape=jax.ShapeDtypeStruct((KV_RANK, N_HEADS * n), BF16),
        scratch_shapes=[
            pltpu.VMEM((seq + qb, qb), F32),
            pltpu.VMEM((1, N_HEADS * qb), F32),
            pltpu.VMEM((CKVT_ROWS, N_HEADS * qb), F32),
            pltpu.VMEM((qb, qb), BF16),
            pltpu.VMEM((1, qb), F32),
        ],
        compiler_params=pltpu.CompilerParams(
            dimension_semantics=("arbitrary", "arbitrary"), vmem_limit_bytes=VMEM_LIMIT),
        name="sparse_attention",
    )(qlat_w, qi_w, wi_w, ckv, ckv_t, ki, d0, d1)


def _mix_kernel(x_ref, ol_ref, u_ref, v_ref, ga_ref, gb_ref, wuv_ref, wpa_ref, ws_ref, bs_ref,
                wpb_ref, wout_ref, g2_ref, wr_ref, br_ref, x1_ref, h2_ref, comb_ref, goh_ref):
    tm = x_ref.shape[0]
    qb = Q_BLK
    o_at = jnp.concatenate(
        [jnp.concatenate(
            [_dot(wuv_ref[h], ol_ref[:, (blk * N_HEADS + h) * qb:(blk * N_HEADS + h + 1) * qb])
             for h in range(N_HEADS)], axis=0)
         for blk in range(tm // qb)], axis=1)
    y_a = _dot(jnp.transpose(o_at).astype(BF16), wpa_ref[...])
    v = v_ref[...]
    u = u_ref[...].astype(F32)
    rows = []
    for c in range(tm // CHUNK):
        cols = []
        for g in range(GMLP_GROUPS):
            vc = v[c * CHUNK:(c + 1) * CHUNK, g * GMLP_GROUP_DIM:(g + 1) * GMLP_GROUP_DIM]
            cols.append(_dot(ws_ref[g], vc) + bs_ref[g])
        rows.append(jnp.concatenate(cols, axis=1))
    s = jnp.concatenate(rows, axis=0)
    y_b = _dot((u * s).astype(BF16), wpb_ref[...])
    merged = ga_ref[...].astype(F32) * y_a + gb_ref[...].astype(F32) * y_b
    x1 = x_ref[...] + _dot(merged.astype(BF16), wout_ref[...])
    x1_ref[...] = x1
    h2 = _rms(x1, g2_ref[...])
    h2_hi = h2.astype(BF16)
    h2_ref[...] = h2_hi

    h2_lo = (h2 - h2_hi.astype(F32)).astype(BF16)
    logits = (_dot(h2_hi, wr_ref[0]) + (_dot(h2_lo, wr_ref[0]) + _dot(h2_hi, wr_ref[1]))
              + br_ref[...])
    lane = lax.broadcasted_iota(jnp.int32, logits.shape, 1)
    big = jnp.int32(4 * LANES)
    is_g = (lane >= N_EXPERTS) & (lane < N_EXPERTS + N_GROUPS)
    lg = jnp.where(is_g, logits, NEG)
    gmax = jnp.max(lg, axis=1, keepdims=True)
    g_w = 1.0 / jnp.sum(jnp.where(is_g, jnp.exp(lg - gmax), 0.0), axis=1, keepdims=True)
    g_idx = jnp.min(jnp.where(is_g & (lg == gmax), lane, big), axis=1, keepdims=True) - N_EXPERTS
    in_g = (lane < N_EXPERTS) & ((lane >> 3) == g_idx)
    le = jnp.where(in_g, logits, NEG)
    emax = jnp.max(le, axis=1, keepdims=True)
    ee = jnp.where(in_g, jnp.exp(le - emax), 0.0)
    ep = ee / jnp.sum(ee, axis=1, keepdims=True)
    ep1 = jnp.where(in_g, ep, -1.0)
    m1 = jnp.max(ep1, axis=1, keepdims=True)
    i1 = jnp.min(jnp.where(ep1 == m1, lane, big), axis=1, keepdims=True)
    ep2 = jnp.where(lane == i1, -1.0, ep1)
    m2 = jnp.max(ep2, axis=1, keepdims=True)
    i2 = jnp.min(jnp.where(ep2 == m2, lane, big), axis=1, keepdims=True)
    tot = m1 + m2
    comb_ref[...] = (jnp.where(lane == i1, g_w * m1 / tot, 0.0)
                     + jnp.where(lane == i2, g_w * m2 / tot, 0.0))
    goh_ref[...] = jnp.transpose(jnp.where(lane == g_idx, 1.0, 0.0))[0:SUBLANES, :]


def _mix(x2, olat_t, u, v, ga, gb, wuv, wpa, ws, bs, wpb, wout, g2, wr, br):
    n, d = x2.shape
    tm = TM_MIX
    row = lambda w: pl.BlockSpec((tm, w), lambda i: (i, 0))
    weights = (wuv, wpa, ws, bs, wpb, wout, g2, wr, br)
    return pl.pallas_call(
        _mix_kernel,
        grid=(n // tm,),
        in_specs=[row(d), pl.BlockSpec((KV_RANK, N_HEADS * tm), lambda i: (0, i)),
                  row(GMLP_WIDTH), row(GMLP_WIDTH), row(d), row(d)]
        + [_const_spec(w.shape) for w in weights],
        out_specs=(row(d), row(d), row(LANES), pl.BlockSpec((SUBLANES, tm), lambda i: (0, i))),
        out_shape=(jax.ShapeDtypeStruct((n, d), F32), jax.ShapeDtypeStruct((n, d), BF16),
                   jax.ShapeDtypeStruct((n, LANES), F32),
                   jax.ShapeDtypeStruct((SUBLANES, n), F32)),
        compiler_params=pltpu.CompilerParams(
            dimension_semantics=("arbitrary",), vmem_limit_bytes=VMEM_LIMIT),
        name="merge_router",
    )(x2, olat_t, u, v, ga, gb, *weights)


def _moe_kernel(x1_ref, h2_ref, comb_ref, goh_ref, tri_ref, wg_ref, wu_ref, wd_ref, gf_ref, o_ref,
                perm_ref, pos_ref, hs_ref, cs_ref, ys_ref, seg_ref):
    step = pl.program_id(1)
    tt, d = hs_ref.shape
    rb = MOE_ROWS
    pc = MOE_PERM_ROWS

    @pl.when(step == 0)
    def _():
        goh = goh_ref[...]
        before = _dot(goh.astype(BF16), tri_ref[...])
        pos = jnp.zeros((1, tt), F32)
        start = jnp.float32(0.0)
        for g in range(N_GROUPS):
            cnt = jnp.sum(goh[g:g + 1, :])
            seg_ref[g] = jnp.floor(start * (1.0 / rb)).astype(jnp.int32)
            seg_ref[N_GROUPS + g] = jnp.ceil((start + cnt) * (1.0 / rb)).astype(jnp.int32)
            pos = pos + goh[g:g + 1, :] * (start + before[g:g + 1, :])
            start = start + cnt
        pos_ref[...] = pos
        for r in range(tt // pc):
            dst = (lax.broadcasted_iota(jnp.int32, (pc, tt), 0) + r * pc).astype(F32)
            perm_ref[r * pc:(r + 1) * pc, :] = jnp.where(pos == dst, 1.0, 0.0).astype(BF16)
        perm = perm_ref[...]
        hs_ref[...] = _dot(perm, h2_ref[...]).astype(BF16)
        comb = comb_ref[...]
        comb_hi = comb.astype(BF16)
        comb_lo = (comb - comb_hi.astype(F32)).astype(BF16)
        cs_ref[...] = _dot(perm, comb_hi) + _dot(perm, comb_lo)
        ys_ref[...] = jnp.zeros(ys_ref.shape, F32)

    group = step // (EXPERTS_PER_GROUP // MOE_EXPERTS_PER_STEP)
    first = seg_ref[group]
    last = seg_ref[N_GROUPS + group]

    def block(b, carry):
        rows = pl.ds(pl.multiple_of(b * rb, rb), rb)
        hb = hs_ref[rows, :]
        cb = cs_ref[rows, :]
        lane = lax.broadcasted_iota(jnp.int32, cb.shape, 1)
        y = jnp.zeros((rb, d), F32)
        for k in range(MOE_EXPERTS_PER_STEP):
            e = step * MOE_EXPERTS_PER_STEP + k
            w_e = jnp.sum(jnp.where(lane == e, cb, 0.0), axis=1, keepdims=True)
            act = jax.nn.silu(_dot(hb, wg_ref[k])) * _dot(hb, wu_ref[k]) * w_e
            y = y + _dot(act.astype(BF16), wd_ref[k])
        ys_ref[rows, :] += y
        return carry

    lax.fori_loop(first, last, block, 0)

    @pl.when(step == pl.num_programs(1) - 1)
    def _():
        ys = ys_ref[...].astype(BF16)
        pos_col = jnp.transpose(jnp.broadcast_to(pos_ref[...], (SUBLANES, tt)))[:, 0:1]
        for r in range(tt // pc):
            rows = slice(r * pc, (r + 1) * pc)
            src = lax.broadcasted_iota(jnp.int32, (pc, tt), 1).astype(F32)
            unperm = jnp.where(pos_col[rows] == src, 1.0, 0.0).astype(BF16)
            o_ref[rows, :] = _rms(x1_ref[rows, :] + _dot(unperm, ys), gf_ref[...])


def _moe(x1, h2, comb, goh_t, wg, wu, wd, gf):
    n, d = x1.shape
    tt = min(TM_MOE, n)
    eps = MOE_EXPERTS_PER_STEP
    assert EXPERTS_PER_GROUP % eps == 0 and tt % MOE_ROWS == 0 and tt % MOE_PERM_ROWS == 0
    row = lambda w: pl.BlockSpec((tt, w), lambda i, s: (i, 0))
    tri = (jnp.arange(tt)[:, None] < jnp.arange(tt)[None, :]).astype(BF16)
    return pl.pallas_call(
        _moe_kernel,
        grid=(n // tt, N_EXPERTS // eps),
        in_specs=[row(d), row(d), row(LANES),
                  pl.BlockSpec((SUBLANES, tt), lambda i, s: (0, i)),
                  _const_spec(tri.shape),
                  pl.BlockSpec((eps, d, D_FF_EXPERT), lambda i, s: (s, 0, 0)),
                  pl.BlockSpec((eps, d, D_FF_EXPERT), lambda i, s: (s, 0, 0)),
                  pl.BlockSpec((eps, D_FF_EXPERT, d), lambda i, s: (s, 0, 0)),
                  _const_spec(gf.shape)],
        out_specs=row(d),
        out_shape=jax.ShapeDtypeStruct((n, d), F32),
        scratch_shapes=[
            pltpu.VMEM((tt, tt), BF16),
            pltpu.VMEM((1, tt), F32),
            pltpu.VMEM((tt, d), BF16),
            pltpu.VMEM((tt, LANES), F32),
            pltpu.VMEM((tt, d), F32),
            pltpu.SMEM((2 * N_GROUPS,), jnp.int32),
        ],
        compiler_params=pltpu.CompilerParams(
            dimension_semantics=("arbitrary", "arbitrary"), vmem_limit_bytes=VMEM_LIMIT),
        name="experts",
    )(x1, h2, comb, goh_t, tri, wg, wu, wd, gf)


def _t5_bucket(n):
    max_exact = NUM_BUCKETS // 2
    nf = jnp.maximum(n, 1).astype(F32)
    large = max_exact + jnp.floor(
        jnp.log(nf / max_exact) / np.float32(np.log(MAX_DISTANCE / max_exact))
        * (NUM_BUCKETS - max_exact)).astype(jnp.int32)
    large = jnp.minimum(large, NUM_BUCKETS - 1)
    return jnp.where(n < max_exact, n, large)


def _bias_tiles(rel_bias):
    qb = Q_BLK
    k = jnp.arange(qb, dtype=jnp.int32)[:, None]
    q = jnp.arange(qb, dtype=jnp.int32)[None, :]
    rel = (rel_bias - rel_bias[NUM_BUCKETS - 1]) * LOG2E

    def tile(dist):
        onehot = (_t5_bucket(dist)[:, :, None] == jnp.arange(NUM_BUCKETS)).astype(F32)
        t = jnp.einsum("kqb,bh->khq", onehot, rel, precision=lax.Precision.HIGHEST)
        return t.reshape(qb, N_HEADS * qb)

    return tile(jnp.maximum(q - k, 0)), tile(qb + q - k)


def _block_diag(w):
    h, a, b = w.shape
    eye = jnp.eye(h, dtype=w.dtype)
    return (eye[:, None, :, None] * w[:, :, None, :]).reshape(h * a, h * b)


def kernel(x, w_in, kv_norm_g, w_uk, w_uv, rel_bias, ln_v_g, ln_v_b, w_spatial, b_spatial,
           w_proj_a, w_proj_b, w_out, norm1_g, norm2_g, router_group_w, router_group_b,
           router_expert_w, router_expert_b, w_gate, w_up, w_down, final_norm_g):
    batch, seq, d = x.shape
    depth = w_in.shape[0]
    n = batch * seq
    assert depth == 1, "the final rms-norm is fused into the (single) layer's expert kernel"
    assert EXPERTS_PER_GROUP == 8
    assert seq % Q_BLK == 0 and n % min(TM_MOE, n) == 0
    assert TM_PROJ % Q_BLK == 0 and n % TM_PROJ == 0 and TM_MIX % Q_BLK == 0 and n % TM_MIX == 0
    assert Q_BLK + 1 >= MAX_DISTANCE

    attn_w = N_HEADS * HEAD_DIM
    sizes = (attn_w, KV_RANK, IDX_HEADS * IDX_DIM, IDX_DIM, IDX_HEADS, GMLP_WIDTH, GMLP_WIDTH, d, d)
    offs = np.concatenate([[0], np.cumsum(sizes)])
    d0, d1 = _bias_tiles(rel_bias)
    tril = jnp.tril(jnp.ones((CHUNK, CHUNK), dtype=bool))
    row = lambda a: a.reshape(1, -1)

    x2 = x.reshape(n, d)
    l = 0
    wl = w_in[l].astype(BF16)
    wq, wc, wqi, wki, wwi, wu, wv, wga, wgb = [wl[:, offs[i]:offs[i + 1]] for i in range(9)]
    wt = jnp.concatenate([wq, wqi, wc, wwi], axis=1).T
    wuk_bd = _block_diag(jnp.transpose(w_uk[l], (1, 0, 2))).astype(BF16)
    wuv_t = jnp.transpose(w_uv[l], (1, 2, 0)).astype(BF16)

    qlat_t, qi_t, wi_t, ckv_t, ckv, ki, u, v, ga, gb = _projection(
        x2, row(norm1_g[l]), wt, wuk_bd, kv_norm_g[l].reshape(-1, 1), wc, row(kv_norm_g[l]),
        wki, wu, wv, row(ln_v_g[l]), row(ln_v_b[l]), wga, wgb)

    olat_t = _attention(qlat_t, qi_t, wi_t, ckv, ckv_t, ki, d0, d1, batch, seq)

    ws = jnp.where(tril[None], w_spatial[l], 0.0).astype(BF16)
    bs = jnp.broadcast_to(b_spatial[l][:, :, None], (GMLP_GROUPS, CHUNK, GMLP_GROUP_DIM))
    wr = jnp.concatenate(
        [router_expert_w[l], router_group_w[l],
         jnp.zeros((d, LANES - N_EXPERTS - N_GROUPS), F32)], axis=1)
    wr_hi = wr.astype(BF16)
    wr = jnp.stack([wr_hi, (wr - wr_hi.astype(F32)).astype(BF16)])
    br = jnp.concatenate(
        [router_expert_b[l], router_group_b[l],
         jnp.zeros((LANES - N_EXPERTS - N_GROUPS,), F32)]).reshape(1, LANES)
    x1, h2, comb, goh_t = _mix(
        x2, olat_t, u, v, ga, gb, wuv_t, w_proj_a[l].astype(BF16), ws, bs,
        w_proj_b[l].astype(BF16), w_out[l].astype(BF16), row(norm2_g[l]), wr, br)

    out = _moe(x1, h2, comb, goh_t, w_gate[l].astype(BF16), w_up[l].astype(BF16),
               w_down[l].astype(BF16), row(final_norm_g))
    return out.reshape(batch, seq, d)
```

```python
import functools

import numpy as np
import jax
import jax.numpy as jnp
from jax import lax
from jax.experimental import pallas as pl
from jax.experimental.pallas import tpu as pltpu

F32 = jnp.float32
BF16 = jnp.bfloat16

N_HEADS = 8
HEAD_DIM = 64
KV_RANK = 128
IDX_HEADS = 8
IDX_DIM = 64
TOPK_MAX = 256
CHUNK = 128
GMLP_GROUPS = 4
GMLP_GROUP_DIM = 128
GMLP_WIDTH = GMLP_GROUPS * GMLP_GROUP_DIM
NUM_BUCKETS = 32
MAX_DISTANCE = 128
N_GROUPS = 4
EXPERTS_PER_GROUP = 8
N_EXPERTS = N_GROUPS * EXPERTS_PER_GROUP
D_FF_EXPERT = 256
EPS = 1e-6

CKVT_ROWS = KV_RANK + 16
LANES = 128
SUBLANES = 8
VMEM_LIMIT = 52 * 1024 * 1024
NEG = -3.0e38
MASK = -1.0e30

TM_PROJ = 512
Q_BLK = 256
TM_MIX = 512
TM_MOE = 1024
MOE_ROWS = 128
MOE_PERM_ROWS = 256
MOE_EXPERTS_PER_STEP = 4
SEARCH_MAX_ITERS = 64
SEARCH_UNROLL = 2
COUNT_ACCS = 4
VERIFY_FROM = 16
VERIFY_EVERY = 4
LOG2E = 1.4426950408889634


def _dot(a, b):
    return jnp.dot(a, b, preferred_element_type=F32)


def _dot_t(a, b):
    return lax.dot_general(a, b, (((1,), (1,)), ((), ())), preferred_element_type=F32)


def _rms(x, g):
    return x * lax.rsqrt(jnp.mean(x * x, axis=-1, keepdims=True) + EPS) * g


_T_ROWS = (N_HEADS * HEAD_DIM, IDX_HEADS * IDX_DIM, KV_RANK, IDX_HEADS)
_T_OFFS = tuple(int(v) for v in np.concatenate([[0], np.cumsum(_T_ROWS)]))


def _proj_kernel(x_ref, g1_ref, wt_ref, wuk_ref, gkvc_ref, wc_ref, gkv_ref, wki_ref, wu_ref,
                 wv_ref, lng_ref, lnb_ref, wga_ref, wgb_ref,
                 qlat_ref, qi_ref, wi_ref, ckvt_ref, ckv_ref, ki_ref, u_ref, v_ref, ga_ref, gb_ref):
    hb = _rms(x_ref[...], g1_ref[...]).astype(BF16)
    t_all = _dot_t(wt_ref[...], hb)
    q_t, qi_t, c_t, w_t = [t_all[_T_OFFS[i]:_T_OFFS[i + 1]] for i in range(4)]
    qlat_t = (_dot(wuk_ref[...], q_t.astype(BF16)) * ((HEAD_DIM ** -0.5) * LOG2E)).astype(BF16)
    w_t = w_t * ((IDX_HEADS ** -0.5) * (IDX_DIM ** -0.5))
    tm = x_ref.shape[0]
    qb = Q_BLK
    for blk in range(tm // qb):
        toks = slice(blk * qb, (blk + 1) * qb)
        for h in range(N_HEADS):
            cols = slice((blk * N_HEADS + h) * qb, (blk * N_HEADS + h + 1) * qb)
            qlat_ref[:, cols] = qlat_t[h * KV_RANK:(h + 1) * KV_RANK, toks]
            qi_ref[:, cols] = qi_t[h * IDX_DIM:(h + 1) * IDX_DIM, toks].astype(BF16)
            wi_ref[:, cols] = w_t[h:h + 1, toks]
    c_n = c_t * lax.rsqrt(jnp.mean(c_t * c_t, axis=0, keepdims=True) + EPS) * gkvc_ref[...]
    ckvt_ref[...] = jnp.concatenate(
        [c_n, jnp.ones((1, tm), F32), jnp.zeros((CKVT_ROWS - KV_RANK - 1, tm), F32)],
        axis=0).astype(BF16)
    ckv_ref[...] = _rms(_dot(hb, wc_ref[...]), gkv_ref[...]).astype(BF16)
    ki_ref[...] = _dot(hb, wki_ref[...]).astype(BF16)
    u_ref[...] = jax.nn.gelu(_dot(hb, wu_ref[...])).astype(BF16)
    v = jax.nn.gelu(_dot(hb, wv_ref[...]))
    mu = jnp.mean(v, axis=-1, keepdims=True)
    var = jnp.mean(jnp.square(v - mu), axis=-1, keepdims=True)
    v_ref[...] = ((v - mu) * lax.rsqrt(var + EPS) * lng_ref[...] + lnb_ref[...]).astype(BF16)
    ga_ref[...] = jax.nn.sigmoid(_dot(hb, wga_ref[...])).astype(BF16)
    gb_ref[...] = jax.nn.sigmoid(_dot(hb, wgb_ref[...])).astype(BF16)


def _const_spec(shape):
    nd = len(shape)
    return pl.BlockSpec(shape, lambda *_: (0,) * nd)


def _projection(x2, g1, wt, wuk, gkvc, wc, gkv, wki, wu, wv, lng, lnb, wga, wgb):
    n, d = x2.shape
    tm = TM_PROJ
    row = lambda w: pl.BlockSpec((tm, w), lambda i: (i, 0))
    col = lambda h: pl.BlockSpec((h, tm), lambda i: (0, i))
    wide = lambda h: pl.BlockSpec((h, N_HEADS * tm), lambda i: (0, i))
    weights = (g1, wt, wuk, gkvc, wc, gkv, wki, wu, wv, lng, lnb, wga, wgb)
    out_shape = (
        jax.ShapeDtypeStruct((KV_RANK, N_HEADS * n), BF16),
        jax.ShapeDtypeStruct((IDX_DIM, IDX_HEADS * n), BF16),
        jax.ShapeDtypeStruct((1, IDX_HEADS * n), F32),
        jax.ShapeDtypeStruct((CKVT_ROWS, n), BF16),
        jax.ShapeDtypeStruct((n, KV_RANK), BF16),
        jax.ShapeDtypeStruct((n, IDX_DIM), BF16),
        jax.ShapeDtypeStruct((n, GMLP_WIDTH), BF16),
        jax.ShapeDtypeStruct((n, GMLP_WIDTH), BF16),
        jax.ShapeDtypeStruct((n, d), BF16),
        jax.ShapeDtypeStruct((n, d), BF16),
    )
    out_specs = (
        wide(KV_RANK), wide(IDX_DIM), wide(1), col(CKVT_ROWS),
        row(KV_RANK), row(IDX_DIM), row(GMLP_WIDTH), row(GMLP_WIDTH), row(d), row(d),
    )
    return pl.pallas_call(
        _proj_kernel,
        grid=(n // tm,),
        in_specs=[row(d)] + [_const_spec(w.shape) for w in weights],
        out_specs=out_specs,
        out_shape=out_shape,
        compiler_params=pltpu.CompilerParams(
            dimension_semantics=("arbitrary",), vmem_limit_bytes=VMEM_LIMIT),
        name="in_projection",
    )(x2, *weights)


def _attn_kernel(qlat_ref, qi_ref, wi_ref, ckv_ref, ckvt_ref, ki_ref, d0_ref, d1_ref, o_ref,
                 idx_ref, m_ref, acc_ref, tri_ref, seen_ref, *, topk, seq):
    qb = Q_BLK
    ns = qb // SUBLANES
    j_q = pl.program_id(1)
    q0 = j_q * qb
    nch = j_q + 1
    kf = float(topk)

    def chunk_rows(j):
        return pl.ds(pl.multiple_of(j * qb, qb), qb)

    krow = lax.broadcasted_iota(jnp.int32, (qb, qb), 0)
    qcol = lax.broadcasted_iota(jnp.int32, (qb, qb), 1)

    def idx_chunk(j, carry):
        rmax, rmin = carry
        zw = jnp.maximum(_dot(ki_ref[chunk_rows(j), :], qi_ref[...]), 0.0) * wi_ref[...]
        index = zw[:, 0:qb]
        for h in range(1, IDX_HEADS):
            index = index + zw[:, h * qb:(h + 1) * qb]
        valid = (j * qb + krow) <= (q0 + qcol)
        x = jnp.where(valid, index, NEG)
        idx_ref[chunk_rows(j), :] = x
        rmax = jnp.maximum(rmax, jnp.max(x, axis=0, keepdims=True))
        rmin = jnp.minimum(rmin, jnp.min(jnp.where(valid, index, -NEG), axis=0, keepdims=True))
        return rmax, rmin

    rmax, rmin = lax.fori_loop(
        0, nch, idx_chunk, (jnp.full((1, qb), NEG, F32), jnp.full((1, qb), -NEG, F32)))

    @pl.when((nch & 1) == 1)
    def _():
        idx_ref[chunk_rows(nch), :] = jnp.full((qb, qb), NEG, F32)

    def slabs(j):
        return idx_ref[chunk_rows(j), :].reshape(ns, SUBLANES, qb)

    def col_sum(c8):
        return jnp.sum(c8, axis=0, keepdims=True)

    def count_ge(thr):
        thr8 = jnp.broadcast_to(thr, (SUBLANES, qb))

        def body(jp, accs):
            pair = idx_ref[pl.ds(pl.multiple_of(jp * (2 * qb), 2 * qb), 2 * qb), :]
            accs = list(accs)
            for s in range(2 * ns):
                xs = pair[s * SUBLANES:(s + 1) * SUBLANES]
                accs[s % COUNT_ACCS] = accs[s % COUNT_ACCS] + jnp.where(xs >= thr8, 1.0, 0.0)
            return tuple(accs)

        zero = jnp.zeros((SUBLANES, qb), F32)
        accs = lax.fori_loop(0, (nch + 1) >> 1, body, (zero,) * COUNT_ACCS)
        return col_sum(functools.reduce(lambda a, b: a + b, accs))

    def bracket_extremes(lo, hi):
        lo8 = jnp.broadcast_to(lo, (SUBLANES, qb))[None]
        hi8 = jnp.broadcast_to(hi, (SUBLANES, qb))[None]

        def body(j, carry):
            bmax, bmin = carry
            xs = slabs(j)
            inb = (xs >= lo8) & (xs < hi8)
            bmax = jnp.maximum(bmax, jnp.max(jnp.where(inb, xs, NEG), axis=0))
            bmin = jnp.minimum(bmin, jnp.min(jnp.where(inb, xs, -NEG), axis=0))
            return bmax, bmin

        bmax, bmin = lax.fori_loop(
            0, nch, body,
            (jnp.full((SUBLANES, qb), NEG, F32), jnp.full((SUBLANES, qb), -NEG, F32)))
        return jnp.max(bmax, axis=0, keepdims=True), jnp.min(bmin, axis=0, keepdims=True)

    def resolve_small_brackets(lo, hi, c_lo, c_hi, tie):
        bmax, bmin = bracket_extremes(lo, hi)
        is_open = c_lo > kf
        top_only = is_open & (bmax != bmin) & ((kf - c_hi) == 1.0)
        lo = jnp.where(top_only, bmax, lo)
        tie = jnp.where(is_open & ((bmax == bmin) | top_only), 1.0, tie)
        return lo, tie

    nvalid = (q0 + 1 + lax.broadcasted_iota(jnp.int32, (1, qb), 1)).astype(F32)
    hi0 = rmax + (jnp.abs(rmax) * 1e-6 + 1e-30)

    def open_rows(c_lo, tie):
        return jnp.max(jnp.where((c_lo > kf) & (tie < 0.5), 1.0, 0.0)) > 0.0

    def search_cond(state):
        it, _, _, c_lo, _, tie = state
        return (it < SEARCH_MAX_ITERS) & open_rows(c_lo, tie)

    def search_body(state):
        it, lo, hi, c_lo, c_hi, tie = state
        for _ in range(SEARCH_UNROLL):
            mid = 0.5 * (lo + hi)
            c = count_ge(mid)
            ok = c >= kf
            lo, hi = jnp.where(ok, mid, lo), jnp.where(ok, hi, mid)
            c_lo, c_hi = jnp.where(ok, c, c_lo), jnp.where(ok, c_hi, c)
        it = it + SEARCH_UNROLL
        check = (it >= VERIFY_FROM) & (((it - VERIFY_FROM) & (VERIFY_EVERY - 1)) == 0)
        lo, tie = lax.cond(check, lambda: resolve_small_brackets(lo, hi, c_lo, c_hi, tie),
                           lambda: (lo, tie))
        return it, lo, hi, c_lo, c_hi, tie

    _, lo, hi, c_lo, c_hi, _ = lax.while_loop(
        search_cond, search_body,
        (jnp.int32(0), rmin, hi0, nvalid, jnp.zeros((1, qb), F32), jnp.zeros((1, qb), F32)))

    need = jnp.where(c_lo > kf, kf - c_hi, float(seq + 1))
    tri_ref[...] = jnp.where(lax.broadcasted_iota(jnp.int32, (qb, qb), 1)
                             <= lax.broadcasted_iota(jnp.int32, (qb, qb), 0), 1.0, 0.0).astype(BF16)
    seen_ref[...] = jnp.zeros(seen_ref.shape, F32)

    m_ref[...] = jnp.full(m_ref.shape, MASK, F32)
    acc_ref[...] = jnp.zeros(acc_ref.shape, F32)

    def att_step(j, bias_ref, nchunks=1):
        rows = pl.ds(pl.multiple_of(j * qb, qb), nchunks * qb)
        ckc = ckv_ref[rows, :]
        ckc_t = ckvt_ref[:, rows]
        xs_all = idx_ref[rows, :]
        seen = seen_ref[...]
        madd = []
        for c in range(nchunks):
            xs = xs_all[c * qb:(c + 1) * qb]
            in_bracket = jnp.where(xs >= lo, jnp.where(xs < hi, 1.0, 0.0), 0.0)
            rank = seen + _dot(tri_ref[...], in_bracket.astype(BF16))
            seen = rank[qb - 1:qb, :]
            madd.append(jnp.where(xs >= hi, 0.0,
                                  jnp.where(xs >= lo, jnp.where(rank <= need, 0.0, MASK), MASK)))
        seen_ref[...] = seen
        madd = madd[0] if nchunks == 1 else jnp.concatenate(madd, axis=0)
        s_all = _dot(ckc, qlat_ref[...])
        s_h = []
        for h in range(N_HEADS):
            s = s_all[:, h * qb:(h + 1) * qb] + madd
            if bias_ref is not None:
                s = s + bias_ref[:, h * qb:(h + 1) * qb]
            s_h.append(s)
        s_all = jnp.concatenate(s_h, axis=1)
        m_old = m_ref[...]
        m_new = jnp.maximum(m_old, jnp.max(s_all, axis=0, keepdims=True))
        alpha = jnp.exp2(m_old - m_new)
        p = jnp.exp2((s_all - m_new).astype(BF16))
        acc_ref[...] = alpha * acc_ref[...] + _dot(ckc_t, p)
        m_ref[...] = m_new

    def far_pair(jp, carry):
        att_step(2 * jp, None, nchunks=2)
        return carry

    nfar = jnp.maximum(nch - 2, 0)
    lax.fori_loop(0, nfar >> 1, far_pair, 0)

    @pl.when((nfar & 1) == 1)
    def _():
        att_step(nfar - 1, None)

    @pl.when(nch >= 2)
    def _():
        att_step(nch - 2, d1_ref)

    att_step(nch - 1, d0_ref)

    o_ref[...] = (acc_ref[0:KV_RANK, :] / acc_ref[KV_RANK:KV_RANK + 1, :]).astype(BF16)


def _attention(qlat_w, qi_w, wi_w, ckv, ckv_t, ki, d0, d1, batch, seq):
    qb = Q_BLK
    nq = seq // qb
    n = batch * seq
    topk = min(TOPK_MAX, seq // 4)
    kern = functools.partial(_attn_kernel, topk=topk, seq=seq)
    wide = lambda h: pl.BlockSpec((h, N_HEADS * qb), lambda b, j: (0, b * nq + j))
    return pl.pallas_call(
        kern,
        grid=(batch, nq),
        in_specs=[
            wide(KV_RANK), wide(IDX_DIM), wide(1),
            pl.BlockSpec((seq, KV_RANK), lambda b, j: (b, 0)),
            pl.BlockSpec((CKVT_ROWS, seq), lambda b, j: (0, b)),
            pl.BlockSpec((seq, IDX_DIM), lambda b, j: (b, 0)),
            _const_spec(d0.shape), _const_spec(d1.shape),
        ],
        out_specs=wide(KV_RANK),
        out_shape=jax.ShapeDtypeStruct((KV_RANK, N_HEADS * n), BF16),
        scratch_shapes=[
            pltpu.VMEM((seq + qb, qb), F32),
            pltpu.VMEM((1, N_HEADS * qb), F32),
            pltpu.VMEM((CKVT_ROWS, N_HEADS * qb), F32),
            pltpu.VMEM((qb, qb), BF16),
            pltpu.VMEM((1, qb), F32),
        ],
        compiler_params=pltpu.CompilerParams(
            dimension_semantics=("arbitrary", "arbitrary"), vmem_limit_bytes=VMEM_LIMIT),
        name="sparse_attention",
    )(qlat_w, qi_w, wi_w, ckv, ckv_t, ki, d0, d1)


def _mix_kernel(x_ref, ol_ref, u_ref, v_ref, ga_ref, gb_ref, wuv_ref, wpa_ref, ws_ref, bs_ref,
                wpb_ref, wout_ref, g2_ref, wr_ref, br_ref, x1_ref, h2_ref, comb_ref, goh_ref):
    tm = x_ref.shape[0]
    qb = Q_BLK
    o_at = jnp.concatenate(
        [jnp.concatenate(
            [_dot(wuv_ref[h], ol_ref[:, (blk * N_HEADS + h) * qb:(blk * N_HEADS + h + 1) * qb])
             for h in range(N_HEADS)], axis=0)
         for blk in range(tm // qb)], axis=1)
    y_a = _dot(jnp.transpose(o_at).astype(BF16), wpa_ref[...])
    v = v_ref[...]
    u = u_ref[...].astype(F32)
    rows = []
    for c in range(tm // CHUNK):
        cols = []
        for g in range(GMLP_GROUPS):
            vc = v[c * CHUNK:(c + 1) * CHUNK, g * GMLP_GROUP_DIM:(g + 1) * GMLP_GROUP_DIM]
            cols.append(_dot(ws_ref[g], vc) + bs_ref[g])
        rows.append(jnp.concatenate(cols, axis=1))
    s = jnp.concatenate(rows, axis=0)
    y_b = _dot((u * s).astype(BF16), wpb_ref[...])
    merged = ga_ref[...].astype(F32) * y_a + gb_ref[...].astype(F32) * y_b
    x1 = x_ref[...] + _dot(merged.astype(BF16), wout_ref[...])
    x1_ref[...] = x1
    h2 = _rms(x1, g2_ref[...])
    h2_hi = h2.astype(BF16)
    h2_ref[...] = h2_hi

    h2_lo = (h2 - h2_hi.astype(F32)).astype(BF16)
    logits = (_dot(h2_hi, wr_ref[0]) + (_dot(h2_lo, wr_ref[0]) + _dot(h2_hi, wr_ref[1]))
              + br_ref[...])
    lane = lax.broadcasted_iota(jnp.int32, logits.shape, 1)
    big = jnp.int32(4 * LANES)
    is_g = (lane >= N_EXPERTS) & (lane < N_EXPERTS + N_GROUPS)
    lg = jnp.where(is_g, logits, NEG)
    gmax = jnp.max(lg, axis=1, keepdims=True)
    g_w = 1.0 / jnp.sum(jnp.where(is_g, jnp.exp(lg - gmax), 0.0), axis=1, keepdims=True)
    g_idx = jnp.min(jnp.where(is_g & (lg == gmax), lane, big), axis=1, keepdims=True) - N_EXPERTS
    in_g = (lane < N_EXPERTS) & ((lane >> 3) == g_idx)
    le = jnp.where(in_g, logits, NEG)
    emax = jnp.max(le, axis=1, keepdims=True)
    ee = jnp.where(in_g, jnp.exp(le - emax), 0.0)
    ep = ee / jnp.sum(ee, axis=1, keepdims=True)
    ep1 = jnp.where(in_g, ep, -1.0)
    m1 = jnp.max(ep1, axis=1, keepdims=True)
    i1 = jnp.min(jnp.where(ep1 == m1, lane, big), axis=1, keepdims=True)
    ep2 = jnp.where(lane == i1, -1.0, ep1)
    m2 = jnp.max(ep2, axis=1, keepdims=True)
    i2 = jnp.min(jnp.where(ep2 == m2, lane, big), axis=1, keepdims=True)
    tot = m1 + m2
    comb_ref[...] = (jnp.where(lane == i1, g_w * m1 / tot, 0.0)
                     + jnp.where(lane == i2, g_w * m2 / tot, 0.0))
    goh_ref[...] = jnp.transpose(jnp.where(lane == g_idx, 1.0, 0.0))[0:SUBLANES, :]


def _mix(x2, olat_t, u, v, ga, gb, wuv, wpa, ws, bs, wpb, wout, g2, wr, br):
    n, d = x2.shape
    tm = TM_MIX
    row = lambda w: pl.BlockSpec((tm, w), lambda i: (i, 0))
    weights = (wuv, wpa, ws, bs, wpb, wout, g2, wr, br)
    return pl.pallas_call(
        _mix_kernel,
        grid=(n // tm,),
        in_specs=[row(d), pl.BlockSpec((KV_RANK, N_HEADS * tm), lambda i: (0, i)),
                  row(GMLP_WIDTH), row(GMLP_WIDTH), row(d), row(d)]
        + [_const_spec(w.shape) for w in weights],
        out_specs=(row(d), row(d), row(LANES), pl.BlockSpec((SUBLANES, tm), lambda i: (0, i))),
        out_shape=(jax.ShapeDtypeStruct((n, d), F32), jax.ShapeDtypeStruct((n, d), BF16),
                   jax.ShapeDtypeStruct((n, LANES), F32),
                   jax.ShapeDtypeStruct((SUBLANES, n), F32)),
        compiler_params=pltpu.CompilerParams(
            dimension_semantics=("arbitrary",), vmem_limit_bytes=VMEM_LIMIT),
        name="merge_router",
    )(x2, olat_t, u, v, ga, gb, *weights)


def _moe_kernel(x1_ref, h2_ref, comb_ref, goh_ref, tri_ref, wg_ref, wu_ref, wd_ref, gf_ref, o_ref,
                perm_ref, pos_ref, hs_ref, cs_ref, ys_ref, seg_ref):
    step = pl.program_id(1)
    tt, d = hs_ref.shape
    rb = MOE_ROWS
    pc = MOE_PERM_ROWS

    @pl.when(step == 0)
    def _():
        goh = goh_ref[...]
        before = _dot(goh.astype(BF16), tri_ref[...])
        pos = jnp.zeros((1, tt), F32)
        start = jnp.float32(0.0)
        for g in range(N_GROUPS):
            cnt = jnp.sum(goh[g:g + 1, :])
            seg_ref[g] = jnp.floor(start * (1.0 / rb)).astype(jnp.int32)
            seg_ref[N_GROUPS + g] = jnp.ceil((start + cnt) * (1.0 / rb)).astype(jnp.int32)
            pos = pos + goh[g:g + 1, :] * (start + before[g:g + 1, :])
            start = start + cnt
        pos_ref[...] = pos
        for r in range(tt // pc):
            dst = (lax.broadcasted_iota(jnp.int32, (pc, tt), 0) + r * pc).astype(F32)
            perm_ref[r * pc:(r + 1) * pc, :] = jnp.where(pos == dst, 1.0, 0.0).astype(BF16)
        perm = perm_ref[...]
        hs_ref[...] = _dot(perm, h2_ref[...]).astype(BF16)
        comb = comb_ref[...]
        comb_hi = comb.astype(BF16)
        comb_lo = (comb - comb_hi.astype(F32)).astype(BF16)
        cs_ref[...] = _dot(perm, comb_hi) + _dot(perm, comb_lo)
        ys_ref[...] = jnp.zeros(ys_ref.shape, F32)

    group = step // (EXPERTS_PER_GROUP // MOE_EXPERTS_PER_STEP)
    first = seg_ref[group]
    last = seg_ref[N_GROUPS + group]

    def block(b, carry):
        rows = pl.ds(pl.multiple_of(b * rb, rb), rb)
        hb = hs_ref[rows, :]
        cb = cs_ref[rows, :]
        lane = lax.broadcasted_iota(jnp.int32, cb.shape, 1)
        y = jnp.zeros((rb, d), F32)
        for k in range(MOE_EXPERTS_PER_STEP):
            e = step * MOE_EXPERTS_PER_STEP + k
            w_e = jnp.sum(jnp.where(lane == e, cb, 0.0), axis=1, keepdims=True)
            act = jax.nn.silu(_dot(hb, wg_ref[k])) * _dot(hb, wu_ref[k]) * w_e
            y = y + _dot(act.astype(BF16), wd_ref[k])
        ys_ref[rows, :] += y
        return carry

    lax.fori_loop(first, last, block, 0)

    @pl.when(step == pl.num_programs(1) - 1)
    def _():
        ys = ys_ref[...].astype(BF16)
        pos_col = jnp.transpose(jnp.broadcast_to(pos_ref[...], (SUBLANES, tt)))[:, 0:1]
        for r in range(tt // pc):
            rows = slice(r * pc, (r + 1) * pc)
            src = lax.broadcasted_iota(jnp.int32, (pc, tt), 1).astype(F32)
            unperm = jnp.where(pos_col[rows] == src, 1.0, 0.0).astype(BF16)
            o_ref[rows, :] = _rms(x1_ref[rows, :] + _dot(unperm, ys), gf_ref[...])


def _moe(x1, h2, comb, goh_t, wg, wu, wd, gf):
    n, d = x1.shape
    tt = min(TM_MOE, n)
    eps = MOE_EXPERTS_PER_STEP
    assert EXPERTS_PER_GROUP % eps == 0 and tt % MOE_ROWS == 0 and tt % MOE_PERM_ROWS == 0
    row = lambda w: pl.BlockSpec((tt, w), lambda i, s: (i, 0))
    tri = (jnp.arange(tt)[:, None] < jnp.arange(tt)[None, :]).astype(BF16)
    return pl.pallas_call(
        _moe_kernel,
        grid=(n // tt, N_EXPERTS // eps),
        in_specs=[row(d), row(d), row(LANES),
                  pl.BlockSpec((SUBLANES, tt), lambda i, s: (0, i)),
                  _const_spec(tri.shape),
                  pl.BlockSpec((eps, d, D_FF_EXPERT), lambda i, s: (s, 0, 0)),
                  pl.BlockSpec((eps, d, D_FF_EXPERT), lambda i, s: (s, 0, 0)),
                  pl.BlockSpec((eps, D_FF_EXPERT, d), lambda i, s: (s, 0, 0)),
                  _const_spec(gf.shape)],
        out_specs=row(d),
        out_shape=jax.ShapeDtypeStruct((n, d), F32),
        scratch_shapes=[
            pltpu.VMEM((tt, tt), BF16),
            pltpu.VMEM((1, tt), F32),
            pltpu.VMEM((tt, d), BF16),
            pltpu.VMEM((tt, LANES), F32),
            pltpu.VMEM((tt, d), F32),
            pltpu.SMEM((2 * N_GROUPS,), jnp.int32),
        ],
        compiler_params=pltpu.CompilerParams(
            dimension_semantics=("arbitrary", "arbitrary"), vmem_limit_bytes=VMEM_LIMIT),
        name="experts",
    )(x1, h2, comb, goh_t, tri, wg, wu, wd, gf)


def _t5_bucket(n):
    max_exact = NUM_BUCKETS // 2
    nf = jnp.maximum(n, 1).astype(F32)
    large = max_exact + jnp.floor(
        jnp.log(nf / max_exact) / np.float32(np.log(MAX_DISTANCE / max_exact))
        * (NUM_BUCKETS - max_exact)).astype(jnp.int32)
    large = jnp.minimum(large, NUM_BUCKETS - 1)
    return jnp.where(n < max_exact, n, large)


def _bias_tiles(rel_bias):
    qb = Q_BLK
    k = jnp.arange(qb, dtype=jnp.int32)[:, None]
    q = jnp.arange(qb, dtype=jnp.int32)[None, :]
    rel = (rel_bias - rel_bias[NUM_BUCKETS - 1]) * LOG2E

    def tile(dist):
        onehot = (_t5_bucket(dist)[:, :, None] == jnp.arange(NUM_BUCKETS)).astype(F32)
        t = jnp.einsum("kqb,bh->khq", onehot, rel, precision=lax.Precision.HIGHEST)
        return t.reshape(qb, N_HEADS * qb)

    return tile(jnp.maximum(q - k, 0)), tile(qb + q - k)


def _block_diag(w):
    h, a, b = w.shape
    eye = jnp.eye(h, dtype=w.dtype)
    return (eye[:, None, :, None] * w[:, :, None, :]).reshape(h * a, h * b)


def kernel(x, w_in, kv_norm_g, w_uk, w_uv, rel_bias, ln_v_g, ln_v_b, w_spatial, b_spatial,
           w_proj_a, w_proj_b, w_out, norm1_g, norm2_g, router_group_w, router_group_b,
           router_expert_w, router_expert_b, w_gate, w_up, w_down, final_norm_g):
    batch, seq, d = x.shape
    depth = w_in.shape[0]
    n = batch * seq
    assert depth == 1, "the final rms-norm is fused into the (single) layer's expert kernel"
    assert EXPERTS_PER_GROUP == 8
    assert seq % Q_BLK == 0 and n % min(TM_MOE, n) == 0
    assert TM_PROJ % Q_BLK == 0 and n % TM_PROJ == 0 and TM_MIX % Q_BLK == 0 and n % TM_MIX == 0
    assert Q_BLK + 1 >= MAX_DISTANCE

    attn_w = N_HEADS * HEAD_DIM
    sizes = (attn_w, KV_RANK, IDX_HEADS * IDX_DIM, IDX_DIM, IDX_HEADS, GMLP_WIDTH, GMLP_WIDTH, d, d)
    offs = np.concatenate([[0], np.cumsum(sizes)])
    d0, d1 = _bias_tiles(rel_bias)
    tril = jnp.tril(jnp.ones((CHUNK, CHUNK), dtype=bool))
    row = lambda a: a.reshape(1, -1)

    x2 = x.reshape(n, d)
    l = 0
    wl = w_in[l].astype(BF16)
    wq, wc, wqi, wki, wwi, wu, wv, wga, wgb = [wl[:, offs[i]:offs[i + 1]] for i in range(9)]
    wt = jnp.concatenate([wq, wqi, wc, wwi], axis=1).T
    wuk_bd = _block_diag(jnp.transpose(w_uk[l], (1, 0, 2))).astype(BF16)
    wuv_t = jnp.transpose(w_uv[l], (1, 2, 0)).astype(BF16)

    qlat_t, qi_t, wi_t, ckv_t, ckv, ki, u, v, ga, gb = _projection(
        x2, row(norm1_g[l]), wt, wuk_bd, kv_norm_g[l].reshape(-1, 1), wc, row(kv_norm_g[l]),
        wki, wu, wv, row(ln_v_g[l]), row(ln_v_b[l]), wga, wgb)

    olat_t = _attention(qlat_t, qi_t, wi_t, ckv, ckv_t, ki, d0, d1, batch, seq)

    ws = jnp.where(tril[None], w_spatial[l], 0.0).astype(BF16)
    bs = jnp.broadcast_to(b_spatial[l][:, :, None], (GMLP_GROUPS, CHUNK, GMLP_GROUP_DIM))
    wr = jnp.concatenate(
        [router_expert_w[l], router_group_w[l],
         jnp.zeros((d, LANES - N_EXPERTS - N_GROUPS), F32)], axis=1)
    wr_hi = wr.astype(BF16)
    wr = jnp.stack([wr_hi, (wr - wr_hi.astype(F32)).astype(BF16)])
    br = jnp.concatenate(
        [router_expert_b[l], router_group_b[l],
         jnp.zeros((LANES - N_EXPERTS - N_GROUPS,), F32)]).reshape(1, LANES)
    x1, h2, comb, goh_t = _mix(
        x2, olat_t, u, v, ga, gb, wuv_t, w_proj_a[l].astype(BF16), ws, bs,
        w_proj_b[l].astype(BF16), w_out[l].astype(BF16), row(norm2_g[l]), wr, br)

    out = _moe(x1, h2, comb, goh_t, w_gate[l].astype(BF16), w_up[l].astype(BF16),
               w_down[l].astype(BF16), row(final_norm_g))
    return out.reshape(batch, seq, d)
```

```python
import functools

import numpy as np
import jax
import jax.numpy as jnp
from jax import lax
from jax.experimental import pallas as pl
from jax.experimental.pallas import tpu as pltpu

F32 = jnp.float32
BF16 = jnp.bfloat16

N_HEADS = 8
HEAD_DIM = 64
KV_RANK = 128
IDX_HEADS = 8
IDX_DIM = 64
TOPK_MAX = 256
CHUNK = 128
GMLP_GROUPS = 4
GMLP_GROUP_DIM = 128
GMLP_WIDTH = GMLP_GROUPS * GMLP_GROUP_DIM
NUM_BUCKETS = 32
MAX_DISTANCE = 128
N_GROUPS = 4
EXPERTS_PER_GROUP = 8
N_EXPERTS = N_GROUPS * EXPERTS_PER_GROUP
D_FF_EXPERT = 256
EPS = 1e-6

CKVT_ROWS = KV_RANK + 16
LANES = 128
SUBLANES = 8
VMEM_LIMIT = 52 * 1024 * 1024
NEG = -3.0e38
MASK = -1.0e30

TM_PROJ = 512
Q_BLK = 256
TM_MIX = 512
TM_MOE = 1024
MOE_ROWS = 128
MOE_PERM_ROWS = 256
MOE_EXPERTS_PER_STEP = 4
SEARCH_MAX_ITERS = 64
SEARCH_UNROLL = 2
COUNT_ACCS = 4
VERIFY_FROM = 16
VERIFY_EVERY = 4
LOG2E = 1.4426950408889634


def _dot(a, b):
    return jnp.dot(a, b, preferred_element_type=F32)


def _dot_t(a, b):
    return lax.dot_general(a, b, (((1,), (1,)), ((), ())), preferred_element_type=F32)


def _rms(x, g):
    return x * lax.rsqrt(jnp.mean(x * x, axis=-1, keepdims=True) + EPS) * g


_T_ROWS = (N_HEADS * HEAD_DIM, IDX_HEADS * IDX_DIM, KV_RANK, IDX_HEADS)
_T_OFFS = tuple(int(v) for v in np.concatenate([[0], np.cumsum(_T_ROWS)]))


def _proj_kernel(x_ref, g1_ref, wt_ref, wuk_ref, gkvc_ref, wc_ref, gkv_ref, wki_ref, wu_ref,
                 wv_ref, lng_ref, lnb_ref, wga_ref, wgb_ref,
                 qlat_ref, qi_ref, wi_ref, ckvt_ref, ckv_ref, ki_ref, u_ref, v_ref, ga_ref, gb_ref):
    hb = _rms(x_ref[...], g1_ref[...]).astype(BF16)
    t_all = _dot_t(wt_ref[...], hb)
    q_t, qi_t, c_t, w_t = [t_all[_T_OFFS[i]:_T_OFFS[i + 1]] for i in range(4)]
    qlat_t = (_dot(wuk_ref[...], q_t.astype(BF16)) * ((HEAD_DIM ** -0.5) * LOG2E)).astype(BF16)
    w_t = w_t * ((IDX_HEADS ** -0.5) * (IDX_DIM ** -0.5))
    tm = x_ref.shape[0]
    qb = Q_BLK
    for blk in range(tm // qb):
        toks = slice(blk * qb, (blk + 1) * qb)
        for h in range(N_HEADS):
            cols = slice((blk * N_HEADS + h) * qb, (blk * N_HEADS + h + 1) * qb)
            qlat_ref[:, cols] = qlat_t[h * KV_RANK:(h + 1) * KV_RANK, toks]
            qi_ref[:, cols] = qi_t[h * IDX_DIM:(h + 1) * IDX_DIM, toks].astype(BF16)
            wi_ref[:, cols] = w_t[h:h + 1, toks]
    c_n = c_t * lax.rsqrt(jnp.mean(c_t * c_t, axis=0, keepdims=True) + EPS) * gkvc_ref[...]
    ckvt_ref[...] = jnp.concatenate(
        [c_n, jnp.ones((1, tm), F32), jnp.zeros((CKVT_ROWS - KV_RANK - 1, tm), F32)],
        axis=0).astype(BF16)
    ckv_ref[...] = _rms(_dot(hb, wc_ref[...]), gkv_ref[...]).astype(BF16)
    ki_ref[...] = _dot(hb, wki_ref[...]).astype(BF16)
    u_ref[...] = jax.nn.gelu(_dot(hb, wu_ref[...])).astype(BF16)
    v = jax.nn.gelu(_dot(hb, wv_ref[...]))
    mu = jnp.mean(v, axis=-1, keepdims=True)
    var = jnp.mean(jnp.square(v - mu), axis=-1, keepdims=True)
    v_ref[...] = ((v - mu) * lax.rsqrt(var + EPS) * lng_ref[...] + lnb_ref[...]).astype(BF16)
    ga_ref[...] = jax.nn.sigmoid(_dot(hb, wga_ref[...])).astype(BF16)
    gb_ref[...] = jax.nn.sigmoid(_dot(hb, wgb_ref[...])).astype(BF16)


def _const_spec(shape):
    nd = len(shape)
    return pl.BlockSpec(shape, lambda *_: (0,) * nd)


def _projection(x2, g1, wt, wuk, gkvc, wc, gkv, wki, wu, wv, lng, lnb, wga, wgb):
    n, d = x2.shape
    tm = TM_PROJ
    row = lambda w: pl.BlockSpec((tm, w), lambda i: (i, 0))
    col = lambda h: pl.BlockSpec((h, tm), lambda i: (0, i))
    wide = lambda h: pl.BlockSpec((h, N_HEADS * tm), lambda i: (0, i))
    weights = (g1, wt, wuk, gkvc, wc, gkv, wki, wu, wv, lng, lnb, wga, wgb)
    out_shape = (
        jax.ShapeDtypeStruct((KV_RANK, N_HEADS * n), BF16),
        jax.ShapeDtypeStruct((IDX_DIM, IDX_HEADS * n), BF16),
        jax.ShapeDtypeStruct((1, IDX_HEADS * n), F32),
        jax.ShapeDtypeStruct((CKVT_ROWS, n), BF16),
        jax.ShapeDtypeStruct((n, KV_RANK), BF16),
        jax.ShapeDtypeStruct((n, IDX_DIM), BF16),
        jax.ShapeDtypeStruct((n, GMLP_WIDTH), BF16),
        jax.ShapeDtypeStruct((n, GMLP_WIDTH), BF16),
        jax.ShapeDtypeStruct((n, d), BF16),
        jax.ShapeDtypeStruct((n, d), BF16),
    )
    out_specs = (
        wide(KV_RANK), wide(IDX_DIM), wide(1), col(CKVT_ROWS),
        row(KV_RANK), row(IDX_DIM), row(GMLP_WIDTH), row(GMLP_WIDTH), row(d), row(d),
    )
    return pl.pallas_call(
        _proj_kernel,
        grid=(n // tm,),
        in_specs=[row(d)] + [_const_spec(w.shape) for w in weights],
        out_specs=out_specs,
        out_shape=out_shape,
        compiler_params=pltpu.CompilerParams(
            dimension_semantics=("arbitrary",), vmem_limit_bytes=VMEM_LIMIT),
        name="in_projection",
    )(x2, *weights)


def _attn_kernel(qlat_ref, qi_ref, wi_ref, ckv_ref, ckvt_ref, ki_ref, d0_ref, d1_ref, o_ref,
                 idx_ref, m_ref, acc_ref, tri_ref, seen_ref, *, topk, seq):
    qb = Q_BLK
    ns = qb // SUBLANES
    j_q = pl.program_id(1)
    q0 = j_q * qb
    nch = j_q + 1
    kf = float(topk)

    def chunk_rows(j):
        return pl.ds(pl.multiple_of(j * qb, qb), qb)

    krow = lax.broadcasted_iota(jnp.int32, (2 * qb, qb), 0)
    qcol = lax.broadcasted_iota(jnp.int32, (2 * qb, qb), 1)

    def idx_pair(jp, carry):
        rmax, rmin = carry
        rows = pl.ds(pl.multiple_of(jp * (2 * qb), 2 * qb), 2 * qb)
        zw = jnp.maximum(_dot(ki_ref[rows, :], qi_ref[...]), 0.0) * wi_ref[...]
        index = zw[:, 0:qb]
        for h in range(1, IDX_HEADS):
            index = index + zw[:, h * qb:(h + 1) * qb]
        valid = (jp * (2 * qb) + krow) <= (q0 + qcol)
        x = jnp.where(valid, index, NEG)
        idx_ref[rows, :] = x
        rmax = jnp.maximum(rmax, jnp.max(x, axis=0, keepdims=True))
        rmin = jnp.minimum(rmin, jnp.min(jnp.where(valid, index, -NEG), axis=0, keepdims=True))
        return rmax, rmin

    rmax, rmin = lax.fori_loop(
        0, (nch + 1) >> 1, idx_pair, (jnp.full((1, qb), NEG, F32), jnp.full((1, qb), -NEG, F32)))

    def slabs(j):
        return idx_ref[chunk_rows(j), :].reshape(ns, SUBLANES, qb)

    def col_sum(c8):
        return jnp.sum(c8, axis=0, keepdims=True)

    def count_ge(thr):
        thr8 = jnp.broadcast_to(thr, (SUBLANES, qb))

        def body(jp, accs):
            pair = idx_ref[pl.ds(pl.multiple_of(jp * (2 * qb), 2 * qb), 2 * qb), :]
            accs = list(accs)
            for s in range(2 * ns):
                xs = pair[s * SUBLANES:(s + 1) * SUBLANES]
                accs[s % COUNT_ACCS] = accs[s % COUNT_ACCS] + jnp.where(xs >= thr8, 1.0, 0.0)
            return tuple(accs)

        zero = jnp.zeros((SUBLANES, qb), F32)
        accs = lax.fori_loop(0, (nch + 1) >> 1, body, (zero,) * COUNT_ACCS)
        return col_sum(functools.reduce(lambda a, b: a + b, accs))

    def bracket_extremes(lo, hi):
        lo8 = jnp.broadcast_to(lo, (SUBLANES, qb))[None]
        hi8 = jnp.broadcast_to(hi, (SUBLANES, qb))[None]

        def body(j, carry):
            bmax, bmin = carry
            xs = slabs(j)
            inb = (xs >= lo8) & (xs < hi8)
            bmax = jnp.maximum(bmax, jnp.max(jnp.where(inb, xs, NEG), axis=0))
            bmin = jnp.minimum(bmin, jnp.min(jnp.where(inb, xs, -NEG), axis=0))
            return bmax, bmin

        bmax, bmin = lax.fori_loop(
            0, nch, body,
            (jnp.full((SUBLANES, qb), NEG, F32), jnp.full((SUBLANES, qb), -NEG, F32)))
        return jnp.max(bmax, axis=0, keepdims=True), jnp.min(bmin, axis=0, keepdims=True)

    def resolve_small_brackets(lo, hi, c_lo, c_hi, tie):
        bmax, bmin = bracket_extremes(lo, hi)
        is_open = c_lo > kf
        top_only = is_open & (bmax != bmin) & ((kf - c_hi) == 1.0)
        lo = jnp.where(top_only, bmax, lo)
        tie = jnp.where(is_open & ((bmax == bmin) | top_only), 1.0, tie)
        return lo, tie

    nvalid = (q0 + 1 + lax.broadcasted_iota(jnp.int32, (1, qb), 1)).astype(F32)
    hi0 = rmax + (jnp.abs(rmax) * 1e-6 + 1e-30)

    def open_rows(c_lo, tie):
        return jnp.max(jnp.where((c_lo > kf) & (tie < 0.5), 1.0, 0.0)) > 0.0

    def search_cond(state):
        it, _, _, c_lo, _, tie = state
        return (it < SEARCH_MAX_ITERS) & open_rows(c_lo, tie)

    def search_body(state):
        it, lo, hi, c_lo, c_hi, tie = state
        for _ in range(SEARCH_UNROLL):
            mid = 0.5 * (lo + hi)
            c = count_ge(mid)
            ok = c >= kf
            lo, hi = jnp.where(ok, mid, lo), jnp.where(ok, hi, mid)
            c_lo, c_hi = jnp.where(ok, c, c_lo), jnp.where(ok, c_hi, c)
        it = it + SEARCH_UNROLL
        check = (it >= VERIFY_FROM) & (((it - VERIFY_FROM) & (VERIFY_EVERY - 1)) == 0)
        lo, tie = lax.cond(check, lambda: resolve_small_brackets(lo, hi, c_lo, c_hi, tie),
                           lambda: (lo, tie))
        return it, lo, hi, c_lo, c_hi, tie

    _, lo, hi, c_lo, c_hi, _ = lax.while_loop(
        search_cond, search_body,
        (jnp.int32(0), rmin, hi0, nvalid, jnp.zeros((1, qb), F32), jnp.zeros((1, qb), F32)))

    need = jnp.where(c_lo > kf, kf - c_hi, float(seq + 1))
    tri_ref[...] = jnp.where(lax.broadcasted_iota(jnp.int32, (qb, qb), 1)
                             <= lax.broadcasted_iota(jnp.int32, (qb, qb), 0), 1.0, 0.0).astype(BF16)
    seen_ref[...] = jnp.zeros(seen_ref.shape, F32)

    m_ref[...] = jnp.full(m_ref.shape, MASK, F32)
    acc_ref[...] = jnp.zeros(acc_ref.shape, F32)

    def att_step(j, bias_ref, nchunks=1):
        rows = pl.ds(pl.multiple_of(j * qb, qb), nchunks * qb)
        ckc = ckv_ref[rows, :]
        ckc_t = ckvt_ref[:, rows]
        xs_all = idx_ref[rows, :]
        seen = seen_ref[...]
        madd = []
        for c in range(nchunks):
            xs = xs_all[c * qb:(c + 1) * qb]
            in_bracket = jnp.where(xs >= lo, jnp.where(xs < hi, 1.0, 0.0), 0.0)
            rank = seen + _dot(tri_ref[...], in_bracket.astype(BF16))
            seen = rank[qb - 1:qb, :]
            madd.append(jnp.where(xs >= hi, 0.0,
                                  jnp.where(xs >= lo, jnp.where(rank <= need, 0.0, MASK), MASK)))
        seen_ref[...] = seen
        madd = madd[0] if nchunks == 1 else jnp.concatenate(madd, axis=0)
        s_all = _dot(ckc, qlat_ref[...])
        s_h = []
        for h in range(N_HEADS):
            s = s_all[:, h * qb:(h + 1) * qb] + madd
            if bias_ref is not None:
                s = s + bias_ref[:, h * qb:(h + 1) * qb]
            s_h.append(s)
        s_all = jnp.concatenate(s_h, axis=1)
        m_old = m_ref[...]
        m_new = jnp.maximum(m_old, jnp.max(s_all, axis=0, keepdims=True))
        alpha = jnp.exp2(m_old - m_new)
        p = jnp.exp2((s_all - m_new).astype(BF16))
        acc_ref[...] = alpha * acc_ref[...] + _dot(ckc_t, p)
        m_ref[...] = m_new

    def far_pair(jp, carry):
        att_step(2 * jp, None, nchunks=2)
        return carry

    nfar = jnp.maximum(nch - 2, 0)
    lax.fori_loop(0, nfar >> 1, far_pair, 0)

    @pl.when((nfar & 1) == 1)
    def _():
        att_step(nfar - 1, None)

    @pl.when(nch >= 2)
    def _():
        att_step(nch - 2, d1_ref)

    att_step(nch - 1, d0_ref)

    o_ref[...] = (acc_ref[0:KV_RANK, :] / acc_ref[KV_RANK:KV_RANK + 1, :]).astype(BF16)


def _attention(qlat_w, qi_w, wi_w, ckv, ckv_t, ki, d0, d1, batch, seq):
    qb = Q_BLK
    nq = seq // qb
    n = batch * seq
    topk = min(TOPK_MAX, seq // 4)
    kern = functools.partial(_attn_kernel, topk=topk, seq=seq)
    wide = lambda h: pl.BlockSpec((h, N_HEADS * qb), lambda b, j: (0, b * nq + j))
    return pl.pallas_call(
        kern,
        grid=(batch, nq),
        in_specs=[
            wide(KV_RANK), wide(IDX_DIM), wide(1),
            pl.BlockSpec((seq, KV_RANK), lambda b, j: (b, 0)),
            pl.BlockSpec((CKVT_ROWS, seq), lambda b, j: (0, b)),
            pl.BlockSpec((seq, IDX_DIM), lambda b, j: (b, 0)),
            _const_spec(d0.shape), _const_spec(d1.shape),
        ],
        out_specs=wide(KV_RANK),
        out_shape=jax.ShapeDtypeStruct((KV_RANK, N_HEADS * n), BF16),
        scratch_shapes=[
            pltpu.VMEM((seq, qb), F32),
            pltpu.VMEM((1, N_HEADS * qb), F32),
            pltpu.VMEM((CKVT_ROWS, N_HEADS * qb), F32),
            pltpu.VMEM((qb, qb), BF16),
            pltpu.VMEM((1, qb), F32),
        ],
        compiler_params=pltpu.CompilerParams(
            dimension_semantics=("arbitrary", "arbitrary"), vmem_limit_bytes=VMEM_LIMIT),
        name="sparse_attention",
    )(qlat_w, qi_w, wi_w, ckv, ckv_t, ki, d0, d1)


def _mix_kernel(x_ref, ol_ref, u_ref, v_ref, ga_ref, gb_ref, wuv_ref, wpa_ref, ws_ref, bs_ref,
                wpb_ref, wout_ref, g2_ref, wr_ref, br_ref, x1_ref, h2_ref, comb_ref, goh_ref):
    tm = x_ref.shape[0]
    qb = Q_BLK
    o_at = jnp.concatenate(
        [jnp.concatenate(
            [_dot(wuv_ref[h], ol_ref[:, (blk * N_HEADS + h) * qb:(blk * N_HEADS + h + 1) * qb])
             for h in range(N_HEADS)], axis=0)
         for blk in range(tm // qb)], axis=1)
    y_a = _dot(jnp.transpose(o_at).astype(BF16), wpa_ref[...])
    v = v_ref[...]
    u = u_ref[...].astype(F32)
    rows = []
    for c in range(tm // CHUNK):
        cols = []
        for g in range(GMLP_GROUPS):
            vc = v[c * CHUNK:(c + 1) * CHUNK, g * GMLP_GROUP_DIM:(g + 1) * GMLP_GROUP_DIM]
            cols.append(_dot(ws_ref[g], vc) + bs_ref[g])
        rows.append(jnp.concatenate(cols, axis=1))
    s = jnp.concatenate(rows, axis=0)
    y_b = _dot((u * s).astype(BF16), wpb_ref[...])
    merged = ga_ref[...].astype(F32) * y_a + gb_ref[...].astype(F32) * y_b
    x1 = x_ref[...] + _dot(merged.astype(BF16), wout_ref[...])
    x1_ref[...] = x1
    h2 = _rms(x1, g2_ref[...])
    h2_hi = h2.astype(BF16)
    h2_ref[...] = h2_hi

    h2_lo = (h2 - h2_hi.astype(F32)).astype(BF16)
    logits = (_dot(h2_hi, wr_ref[0]) + (_dot(h2_lo, wr_ref[0]) + _dot(h2_hi, wr_ref[1]))
              + br_ref[...])
    lane = lax.broadcasted_iota(jnp.int32, logits.shape, 1)
    big = jnp.int32(4 * LANES)
    is_g = (lane >= N_EXPERTS) & (lane < N_EXPERTS + N_GROUPS)
    lg = jnp.where(is_g, logits, NEG)
    gmax = jnp.max(lg, axis=1, keepdims=True)
    g_w = 1.0 / jnp.sum(jnp.where(is_g, jnp.exp(lg - gmax), 0.0), axis=1, keepdims=True)
    g_idx = jnp.min(jnp.where(is_g & (lg == gmax), lane, big), axis=1, keepdims=True) - N_EXPERTS
    in_g = (lane < N_EXPERTS) & ((lane >> 3) == g_idx)
    le = jnp.where(in_g, logits, NEG)
    emax = jnp.max(le, axis=1, keepdims=True)
    ee = jnp.where(in_g, jnp.exp(le - emax), 0.0)
    ep = ee / jnp.sum(ee, axis=1, keepdims=True)
    ep1 = jnp.where(in_g, ep, -1.0)
    m1 = jnp.max(ep1, axis=1, keepdims=True)
    i1 = jnp.min(jnp.where(ep1 == m1, lane, big), axis=1, keepdims=True)
    ep2 = jnp.where(lane == i1, -1.0, ep1)
    m2 = jnp.max(ep2, axis=1, keepdims=True)
    i2 = jnp.min(jnp.where(ep2 == m2, lane, big), axis=1, keepdims=True)
    tot = m1 + m2
    comb_ref[...] = (jnp.where(lane == i1, g_w * m1 / tot, 0.0)
                     + jnp.where(lane == i2, g_w * m2 / tot, 0.0))
    goh_ref[...] = jnp.transpose(jnp.where(lane == g_idx, 1.0, 0.0))[0:SUBLANES, :]


def _mix(x2, olat_t, u, v, ga, gb, wuv, wpa, ws, bs, wpb, wout, g2, wr, br):
    n, d = x2.shape
    tm = TM_MIX
    row = lambda w: pl.BlockSpec((tm, w), lambda i: (i, 0))
    weights = (wuv, wpa, ws, bs, wpb, wout, g2, wr, br)
    return pl.pallas_call(
        _mix_kernel,
        grid=(n // tm,),
        in_specs=[row(d), pl.BlockSpec((KV_RANK, N_HEADS * tm), lambda i: (0, i)),
                  row(GMLP_WIDTH), row(GMLP_WIDTH), row(d), row(d)]
        + [_const_spec(w.shape) for w in weights],
        out_specs=(row(d), row(d), row(LANES), pl.BlockSpec((SUBLANES, tm), lambda i: (0, i))),
        out_shape=(jax.ShapeDtypeStruct((n, d), F32), jax.ShapeDtypeStruct((n, d), BF16),
                   jax.ShapeDtypeStruct((n, LANES), F32),
                   jax.ShapeDtypeStruct((SUBLANES, n), F32)),
        compiler_params=pltpu.CompilerParams(
            dimension_semantics=("arbitrary",), vmem_limit_bytes=VMEM_LIMIT),
        name="merge_router",
    )(x2, olat_t, u, v, ga, gb, *weights)


def _moe_kernel(x1_ref, h2_ref, comb_ref, goh_ref, tri_ref, wg_ref, wu_ref, wd_ref, gf_ref, o_ref,
                perm_ref, pos_ref, hs_ref, cs_ref, ys_ref, seg_ref):
    step = pl.program_id(1)
    tt, d = hs_ref.shape
    rb = MOE_ROWS
    pc = MOE_PERM_ROWS

    @pl.when(step == 0)
    def _():
        goh = goh_ref[...]
        before = _dot(goh.astype(BF16), tri_ref[...])
        pos = jnp.zeros((1, tt), F32)
        start = jnp.float32(0.0)
        for g in range(N_GROUPS):
            cnt = jnp.sum(goh[g:g + 1, :])
            seg_ref[g] = jnp.floor(start * (1.0 / rb)).astype(jnp.int32)
            seg_ref[N_GROUPS + g] = jnp.ceil((start + cnt) * (1.0 / rb)).astype(jnp.int32)
            pos = pos + goh[g:g + 1, :] * (start + before[g:g + 1, :])
            start = start + cnt
        pos_ref[...] = pos
        for r in range(tt // pc):
            dst = (lax.broadcasted_iota(jnp.int32, (pc, tt), 0) + r * pc).astype(F32)
            perm_ref[r * pc:(r + 1) * pc, :] = jnp.where(pos == dst, 1.0, 0.0).astype(BF16)
        perm = perm_ref[...]
        hs_ref[...] = _dot(perm, h2_ref[...]).astype(BF16)
        comb = comb_ref[...]
        comb_hi = comb.astype(BF16)
        comb_lo = (comb - comb_hi.astype(F32)).astype(BF16)
        cs_ref[...] = _dot(perm, comb_hi) + _dot(perm, comb_lo)
        ys_ref[...] = jnp.zeros(ys_ref.shape, F32)

    group = step // (EXPERTS_PER_GROUP // MOE_EXPERTS_PER_STEP)
    first = seg_ref[group]
    last = seg_ref[N_GROUPS + group]

    def block(b, carry):
        rows = pl.ds(pl.multiple_of(b * rb, rb), rb)
        hb = hs_ref[rows, :]
        cb = cs_ref[rows, :]
        lane = lax.broadcasted_iota(jnp.int32, cb.shape, 1)
        y = jnp.zeros((rb, d), F32)
        for k in range(MOE_EXPERTS_PER_STEP):
            e = step * MOE_EXPERTS_PER_STEP + k
            w_e = jnp.sum(jnp.where(lane == e, cb, 0.0), axis=1, keepdims=True)
            act = jax.nn.silu(_dot(hb, wg_ref[k])) * _dot(hb, wu_ref[k]) * w_e
            y = y + _dot(act.astype(BF16), wd_ref[k])
        ys_ref[rows, :] += y
        return carry

    lax.fori_loop(first, last, block, 0)

    @pl.when(step == pl.num_programs(1) - 1)
    def _():
        ys = ys_ref[...].astype(BF16)
        pos_col = jnp.transpose(jnp.broadcast_to(pos_ref[...], (SUBLANES, tt)))[:, 0:1]
        for r in range(tt // pc):
            rows = slice(r * pc, (r + 1) * pc)
            src = lax.broadcasted_iota(jnp.int32, (pc, tt), 1).astype(F32)
            unperm = jnp.where(pos_col[rows] == src, 1.0, 0.0).astype(BF16)
            o_ref[rows, :] = _rms(x1_ref[rows, :] + _dot(unperm, ys), gf_ref[...])


def _moe(x1, h2, comb, goh_t, wg, wu, wd, gf):
    n, d = x1.shape
    tt = min(TM_MOE, n)
    eps = MOE_EXPERTS_PER_STEP
    assert EXPERTS_PER_GROUP % eps == 0 and tt % MOE_ROWS == 0 and tt % MOE_PERM_ROWS == 0
    row = lambda w: pl.BlockSpec((tt, w), lambda i, s: (i, 0))
    tri = (jnp.arange(tt)[:, None] < jnp.arange(tt)[None, :]).astype(BF16)
    return pl.pallas_call(
        _moe_kernel,
        grid=(n // tt, N_EXPERTS // eps),
        in_specs=[row(d), row(d), row(LANES),
                  pl.BlockSpec((SUBLANES, tt), lambda i, s: (0, i)),
                  _const_spec(tri.shape),
                  pl.BlockSpec((eps, d, D_FF_EXPERT), lambda i, s: (s, 0, 0)),
                  pl.BlockSpec((eps, d, D_FF_EXPERT), lambda i, s: (s, 0, 0)),
                  pl.BlockSpec((eps, D_FF_EXPERT, d), lambda i, s: (s, 0, 0)),
                  _const_spec(gf.shape)],
        out_specs=row(d),
        out_shape=jax.ShapeDtypeStruct((n, d), F32),
        scratch_shapes=[
            pltpu.VMEM((tt, tt), BF16),
            pltpu.VMEM((1, tt), F32),
            pltpu.VMEM((tt, d), BF16),
            pltpu.VMEM((tt, LANES), F32),
            pltpu.VMEM((tt, d), F32),
            pltpu.SMEM((2 * N_GROUPS,), jnp.int32),
        ],
        compiler_params=pltpu.CompilerParams(
            dimension_semantics=("arbitrary", "arbitrary"), vmem_limit_bytes=VMEM_LIMIT),
        name="experts",
    )(x1, h2, comb, goh_t, tri, wg, wu, wd, gf)


def _t5_bucket(n):
    max_exact = NUM_BUCKETS // 2
    nf = jnp.maximum(n, 1).astype(F32)
    large = max_exact + jnp.floor(
        jnp.log(nf / max_exact) / np.float32(np.log(MAX_DISTANCE / max_exact))
        * (NUM_BUCKETS - max_exact)).astype(jnp.int32)
    large = jnp.minimum(large, NUM_BUCKETS - 1)
    return jnp.where(n < max_exact, n, large)


def _bias_tiles(rel_bias):
    qb = Q_BLK
    k = jnp.arange(qb, dtype=jnp.int32)[:, None]
    q = jnp.arange(qb, dtype=jnp.int32)[None, :]
    rel = (rel_bias - rel_bias[NUM_BUCKETS - 1]) * LOG2E

    def tile(dist):
        onehot = (_t5_bucket(dist)[:, :, None] == jnp.arange(NUM_BUCKETS)).astype(F32)
        t = jnp.einsum("kqb,bh->khq", onehot, rel, precision=lax.Precision.HIGHEST)
        return t.reshape(qb, N_HEADS * qb)

    return tile(jnp.maximum(q - k, 0)), tile(qb + q - k)


def _block_diag(w):
    h, a, b = w.shape
    eye = jnp.eye(h, dtype=w.dtype)
    return (eye[:, None, :, None] * w[:, :, None, :]).reshape(h * a, h * b)


def kernel(x, w_in, kv_norm_g, w_uk, w_uv, rel_bias, ln_v_g, ln_v_b, w_spatial, b_spatial,
           w_proj_a, w_proj_b, w_out, norm1_g, norm2_g, router_group_w, router_group_b,
           router_expert_w, router_expert_b, w_gate, w_up, w_down, final_norm_g):
    batch, seq, d = x.shape
    depth = w_in.shape[0]
    n = batch * seq
    assert depth == 1, "the final rms-norm is fused into the (single) layer's expert kernel"
    assert EXPERTS_PER_GROUP == 8
    assert seq % (2 * Q_BLK) == 0 and n % min(TM_MOE, n) == 0
    assert TM_PROJ % Q_BLK == 0 and n % TM_PROJ == 0 and TM_MIX % Q_BLK == 0 and n % TM_MIX == 0
    assert Q_BLK + 1 >= MAX_DISTANCE

    attn_w = N_HEADS * HEAD_DIM
    sizes = (attn_w, KV_RANK, IDX_HEADS * IDX_DIM, IDX_DIM, IDX_HEADS, GMLP_WIDTH, GMLP_WIDTH, d, d)
    offs = np.concatenate([[0], np.cumsum(sizes)])
    d0, d1 = _bias_tiles(rel_bias)
    tril = jnp.tril(jnp.ones((CHUNK, CHUNK), dtype=bool))
    row = lambda a: a.reshape(1, -1)

    x2 = x.reshape(n, d)
    l = 0
    wl = w_in[l].astype(BF16)
    wq, wc, wqi, wki, wwi, wu, wv, wga, wgb = [wl[:, offs[i]:offs[i + 1]] for i in range(9)]
    wt = jnp.concatenate([wq, wqi, wc, wwi], axis=1).T
    wuk_bd = _block_diag(jnp.transpose(w_uk[l], (1, 0, 2))).astype(BF16)
    wuv_t = jnp.transpose(w_uv[l], (1, 2, 0)).astype(BF16)

    qlat_t, qi_t, wi_t, ckv_t, ckv, ki, u, v, ga, gb = _projection(
        x2, row(norm1_g[l]), wt, wuk_bd, kv_norm_g[l].reshape(-1, 1), wc, row(kv_norm_g[l]),
        wki, wu, wv, row(ln_v_g[l]), row(ln_v_b[l]), wga, wgb)

    olat_t = _attention(qlat_t, qi_t, wi_t, ckv, ckv_t, ki, d0, d1, batch, seq)

    ws = jnp.where(tril[None], w_spatial[l], 0.0).astype(BF16)
    bs = jnp.broadcast_to(b_spatial[l][:, :, None], (GMLP_GROUPS, CHUNK, GMLP_GROUP_DIM))
    wr = jnp.concatenate(
        [router_expert_w[l], router_group_w[l],
         jnp.zeros((d, LANES - N_EXPERTS - N_GROUPS), F32)], axis=1)
    wr_hi = wr.astype(BF16)
    wr = jnp.stack([wr_hi, (wr - wr_hi.astype(F32)).astype(BF16)])
    br = jnp.concatenate(
        [router_expert_b[l], router_group_b[l],
         jnp.zeros((LANES - N_EXPERTS - N_GROUPS,), F32)]).reshape(1, LANES)
    x1, h2, comb, goh_t = _mix(
        x2, olat_t, u, v, ga, gb, wuv_t, w_proj_a[l].astype(BF16), ws, bs,
        w_proj_b[l].astype(BF16), w_out[l].astype(BF16), row(norm2_g[l]), wr, br)

    out = _moe(x1, h2, comb, goh_t, w_gate[l].astype(BF16), w_up[l].astype(BF16),
               w_down[l].astype(BF16), row(final_norm_g))
    return out.reshape(batch, seq, d)
```

```python
import functools

import numpy as np
import jax
import jax.numpy as jnp
from jax import lax
from jax.experimental import pallas as pl
from jax.experimental.pallas import tpu as pltpu

F32 = jnp.float32
BF16 = jnp.bfloat16

N_HEADS = 8
HEAD_DIM = 64
KV_RANK = 128
IDX_HEADS = 8
IDX_DIM = 64
TOPK_MAX = 256
CHUNK = 128
GMLP_GROUPS = 4
GMLP_GROUP_DIM = 128
GMLP_WIDTH = GMLP_GROUPS * GMLP_GROUP_DIM
NUM_BUCKETS = 32
MAX_DISTANCE = 128
N_GROUPS = 4
EXPERTS_PER_GROUP = 8
N_EXPERTS = N_GROUPS * EXPERTS_PER_GROUP
D_FF_EXPERT = 256
EPS = 1e-6

CKVT_ROWS = KV_RANK + 16
LANES = 128
SUBLANES = 8
VMEM_LIMIT = 52 * 1024 * 1024
NEG = -3.0e38
MASK = -1.0e30

TM_PROJ = 512
Q_BLK = 256
TM_MIX = 512
TM_MOE = 1024
MOE_ROWS = 128
MOE_PERM_ROWS = 256
MOE_EXPERTS_PER_STEP = 4
SEARCH_MAX_ITERS = 64
SEARCH_UNROLL = 4
COUNT_ACCS = 4
VERIFY_FROM = 16
VERIFY_EVERY = 4
LOG2E = 1.4426950408889634


def _dot(a, b):
    return jnp.dot(a, b, preferred_element_type=F32)


def _dot_t(a, b):
    return lax.dot_general(a, b, (((1,), (1,)), ((), ())), preferred_element_type=F32)


def _rms(x, g):
    return x * lax.rsqrt(jnp.mean(x * x, axis=-1, keepdims=True) + EPS) * g


_T_ROWS = (N_HEADS * HEAD_DIM, IDX_HEADS * IDX_DIM, KV_RANK, IDX_HEADS)
_T_OFFS = tuple(int(v) for v in np.concatenate([[0], np.cumsum(_T_ROWS)]))


def _proj_kernel(x_ref, g1_ref, wt_ref, wuk_ref, gkvc_ref, wc_ref, gkv_ref, wki_ref, wu_ref,
                 wv_ref, lng_ref, lnb_ref, wga_ref, wgb_ref,
                 qlat_ref, qi_ref, wi_ref, ckvt_ref, ckv_ref, ki_ref, u_ref, v_ref, ga_ref, gb_ref):
    hb = _rms(x_ref[...], g1_ref[...]).astype(BF16)
    t_all = _dot_t(wt_ref[...], hb)
    q_t, qi_t, c_t, w_t = [t_all[_T_OFFS[i]:_T_OFFS[i + 1]] for i in range(4)]
    qlat_t = (_dot(wuk_ref[...], q_t.astype(BF16)) * ((HEAD_DIM ** -0.5) * LOG2E)).astype(BF16)
    w_t = w_t * ((IDX_HEADS ** -0.5) * (IDX_DIM ** -0.5))
    tm = x_ref.shape[0]
    qb = Q_BLK
    for blk in range(tm // qb):
        toks = slice(blk * qb, (blk + 1) * qb)
        for h in range(N_HEADS):
            cols = slice((blk * N_HEADS + h) * qb, (blk * N_HEADS + h + 1) * qb)
            qlat_ref[:, cols] = qlat_t[h * KV_RANK:(h + 1) * KV_RANK, toks]
            qi_ref[:, cols] = qi_t[h * IDX_DIM:(h + 1) * IDX_DIM, toks].astype(BF16)
            wi_ref[:, cols] = w_t[h:h + 1, toks]
    c_n = c_t * lax.rsqrt(jnp.mean(c_t * c_t, axis=0, keepdims=True) + EPS) * gkvc_ref[...]
    ckvt_ref[...] = jnp.concatenate(
        [c_n, jnp.ones((1, tm), F32), jnp.zeros((CKVT_ROWS - KV_RANK - 1, tm), F32)],
        axis=0).astype(BF16)
    ckv_ref[...] = _rms(_dot(hb, wc_ref[...]), gkv_ref[...]).astype(BF16)
    ki_ref[...] = _dot(hb, wki_ref[...]).astype(BF16)
    u_ref[...] = jax.nn.gelu(_dot(hb, wu_ref[...])).astype(BF16)
    v = jax.nn.gelu(_dot(hb, wv_ref[...]))
    mu = jnp.mean(v, axis=-1, keepdims=True)
    var = jnp.mean(jnp.square(v - mu), axis=-1, keepdims=True)
    v_ref[...] = ((v - mu) * lax.rsqrt(var + EPS) * lng_ref[...] + lnb_ref[...]).astype(BF16)
    ga_ref[...] = jax.nn.sigmoid(_dot(hb, wga_ref[...])).astype(BF16)
    gb_ref[...] = jax.nn.sigmoid(_dot(hb, wgb_ref[...])).astype(BF16)


def _const_spec(shape):
    nd = len(shape)
    return pl.BlockSpec(shape, lambda *_: (0,) * nd)


def _projection(x2, g1, wt, wuk, gkvc, wc, gkv, wki, wu, wv, lng, lnb, wga, wgb):
    n, d = x2.shape
    tm = TM_PROJ
    row = lambda w: pl.BlockSpec((tm, w), lambda i: (i, 0))
    col = lambda h: pl.BlockSpec((h, tm), lambda i: (0, i))
    wide = lambda h: pl.BlockSpec((h, N_HEADS * tm), lambda i: (0, i))
    weights = (g1, wt, wuk, gkvc, wc, gkv, wki, wu, wv, lng, lnb, wga, wgb)
    out_shape = (
        jax.ShapeDtypeStruct((KV_RANK, N_HEADS * n), BF16),
        jax.ShapeDtypeStruct((IDX_DIM, IDX_HEADS * n), BF16),
        jax.ShapeDtypeStruct((1, IDX_HEADS * n), F32),
        jax.ShapeDtypeStruct((CKVT_ROWS, n), BF16),
        jax.ShapeDtypeStruct((n, KV_RANK), BF16),
        jax.ShapeDtypeStruct((n, IDX_DIM), BF16),
        jax.ShapeDtypeStruct((n, GMLP_WIDTH), BF16),
        jax.ShapeDtypeStruct((n, GMLP_WIDTH), BF16),
        jax.ShapeDtypeStruct((n, d), BF16),
        jax.ShapeDtypeStruct((n, d), BF16),
    )
    out_specs = (
        wide(KV_RANK), wide(IDX_DIM), wide(1), col(CKVT_ROWS),
        row(KV_RANK), row(IDX_DIM), row(GMLP_WIDTH), row(GMLP_WIDTH), row(d), row(d),
    )
    return pl.pallas_call(
        _proj_kernel,
        grid=(n // tm,),
        in_specs=[row(d)] + [_const_spec(w.shape) for w in weights],
        out_specs=out_specs,
        out_shape=out_shape,
        compiler_params=pltpu.CompilerParams(
            dimension_semantics=("arbitrary",), vmem_limit_bytes=VMEM_LIMIT),
        name="in_projection",
    )(x2, *weights)


def _attn_kernel(qlat_ref, qi_ref, wi_ref, ckv_ref, ckvt_ref, ki_ref, d0_ref, d1_ref, o_ref,
                 idx_ref, m_ref, acc_ref, tri_ref, seen_ref, *, topk, seq):
    qb = Q_BLK
    ns = qb // SUBLANES
    j_q = pl.program_id(1)
    q0 = j_q * qb
    nch = j_q + 1
    kf = float(topk)

    def chunk_rows(j):
        return pl.ds(pl.multiple_of(j * qb, qb), qb)

    krow = lax.broadcasted_iota(jnp.int32, (2 * qb, qb), 0)
    qcol = lax.broadcasted_iota(jnp.int32, (2 * qb, qb), 1)

    def idx_pair(jp, carry):
        rmax, rmin = carry
        rows = pl.ds(pl.multiple_of(jp * (2 * qb), 2 * qb), 2 * qb)
        zw = jnp.maximum(_dot(ki_ref[rows, :], qi_ref[...]), 0.0) * wi_ref[...]
        index = zw[:, 0:qb]
        for h in range(1, IDX_HEADS):
            index = index + zw[:, h * qb:(h + 1) * qb]
        valid = (jp * (2 * qb) + krow) <= (q0 + qcol)
        x = jnp.where(valid, index, NEG)
        idx_ref[rows, :] = x
        rmax = jnp.maximum(rmax, jnp.max(x, axis=0, keepdims=True))
        rmin = jnp.minimum(rmin, jnp.min(jnp.where(valid, index, -NEG), axis=0, keepdims=True))
        return rmax, rmin

    rmax, rmin = lax.fori_loop(
        0, (nch + 1) >> 1, idx_pair, (jnp.full((1, qb), NEG, F32), jnp.full((1, qb), -NEG, F32)))

    def slabs(j):
        return idx_ref[chunk_rows(j), :].reshape(ns, SUBLANES, qb)

    def col_sum(c8):
        return jnp.sum(c8, axis=0, keepdims=True)

    def count_ge(thr):
        thr8 = jnp.broadcast_to(thr, (SUBLANES, qb))

        def body(jp, accs):
            pair = idx_ref[pl.ds(pl.multiple_of(jp * (2 * qb), 2 * qb), 2 * qb), :]
            accs = list(accs)
            for s in range(2 * ns):
                xs = pair[s * SUBLANES:(s + 1) * SUBLANES]
                accs[s % COUNT_ACCS] = accs[s % COUNT_ACCS] + jnp.where(xs >= thr8, 1.0, 0.0)
            return tuple(accs)

        zero = jnp.zeros((SUBLANES, qb), F32)
        accs = lax.fori_loop(0, (nch + 1) >> 1, body, (zero,) * COUNT_ACCS)
        return col_sum(functools.reduce(lambda a, b: a + b, accs))

    def bracket_extremes(lo, hi):
        lo8 = jnp.broadcast_to(lo, (SUBLANES, qb))[None]
        hi8 = jnp.broadcast_to(hi, (SUBLANES, qb))[None]

        def body(j, carry):
            bmax, bmin = carry
            xs = slabs(j)
            inb = (xs >= lo8) & (xs < hi8)
            bmax = jnp.maximum(bmax, jnp.max(jnp.where(inb, xs, NEG), axis=0))
            bmin = jnp.minimum(bmin, jnp.min(jnp.where(inb, xs, -NEG), axis=0))
            return bmax, bmin

        bmax, bmin = lax.fori_loop(
            0, nch, body,
            (jnp.full((SUBLANES, qb), NEG, F32), jnp.full((SUBLANES, qb), -NEG, F32)))
        return jnp.max(bmax, axis=0, keepdims=True), jnp.min(bmin, axis=0, keepdims=True)

    def resolve_small_brackets(lo, hi, c_lo, c_hi, tie):
        bmax, bmin = bracket_extremes(lo, hi)
        is_open = c_lo > kf
        top_only = is_open & (bmax != bmin) & ((kf - c_hi) == 1.0)
        lo = jnp.where(top_only, bmax, lo)
        tie = jnp.where(is_open & ((bmax == bmin) | top_only), 1.0, tie)
        return lo, tie

    nvalid = (q0 + 1 + lax.broadcasted_iota(jnp.int32, (1, qb), 1)).astype(F32)
    hi0 = rmax + (jnp.abs(rmax) * 1e-6 + 1e-30)

    def open_rows(c_lo, tie):
        return jnp.max(jnp.where((c_lo > kf) & (tie < 0.5), 1.0, 0.0)) > 0.0

    def search_cond(state):
        it, _, _, c_lo, _, tie = state
        return (it < SEARCH_MAX_ITERS) & open_rows(c_lo, tie)

    def search_body(state):
        it, lo, hi, c_lo, c_hi, tie = state
        for _ in range(SEARCH_UNROLL):
            mid = 0.5 * (lo + hi)
            c = count_ge(mid)
            ok = c >= kf
            lo, hi = jnp.where(ok, mid, lo), jnp.where(ok, hi, mid)
            c_lo, c_hi = jnp.where(ok, c, c_lo), jnp.where(ok, c_hi, c)
        it = it + SEARCH_UNROLL
        check = (it >= VERIFY_FROM) & (((it - VERIFY_FROM) & (VERIFY_EVERY - 1)) == 0)
        lo, tie = lax.cond(check, lambda: resolve_small_brackets(lo, hi, c_lo, c_hi, tie),
                           lambda: (lo, tie))
        return it, lo, hi, c_lo, c_hi, tie

    _, lo, hi, c_lo, c_hi, _ = lax.while_loop(
        search_cond, search_body,
        (jnp.int32(0), rmin, hi0, nvalid, jnp.zeros((1, qb), F32), jnp.zeros((1, qb), F32)))

    need = jnp.where(c_lo > kf, kf - c_hi, float(seq + 1))
    tri_ref[...] = jnp.where(lax.broadcasted_iota(jnp.int32, (qb, qb), 1)
                             <= lax.broadcasted_iota(jnp.int32, (qb, qb), 0), 1.0, 0.0).astype(BF16)
    seen_ref[...] = jnp.zeros(seen_ref.shape, F32)

    m_ref[...] = jnp.full(m_ref.shape, MASK, F32)
    acc_ref[...] = jnp.zeros(acc_ref.shape, F32)

    def att_step(j, bias_ref, nchunks=1):
        rows = pl.ds(pl.multiple_of(j * qb, qb), nchunks * qb)
        ckc = ckv_ref[rows, :]
        ckc_t = ckvt_ref[:, rows]
        xs_all = idx_ref[rows, :]
        seen = seen_ref[...]
        madd = []
        for c in range(nchunks):
            xs = xs_all[c * qb:(c + 1) * qb]
            in_bracket = jnp.where(xs >= lo, jnp.where(xs < hi, 1.0, 0.0), 0.0)
            rank = seen + _dot(tri_ref[...], in_bracket.astype(BF16))
            seen = rank[qb - 1:qb, :]
            madd.append(jnp.where(xs >= hi, 0.0,
                                  jnp.where(xs >= lo, jnp.where(rank <= need, 0.0, MASK), MASK)))
        seen_ref[...] = seen
        madd = madd[0] if nchunks == 1 else jnp.concatenate(madd, axis=0)
        s_all = _dot(ckc, qlat_ref[...])
        s_h = []
        for h in range(N_HEADS):
            s = s_all[:, h * qb:(h + 1) * qb] + madd
            if bias_ref is not None:
                s = s + bias_ref[:, h * qb:(h + 1) * qb]
            s_h.append(s)
        s_all = jnp.concatenate(s_h, axis=1)
        m_old = m_ref[...]
        m_new = jnp.maximum(m_old, jnp.max(s_all, axis=0, keepdims=True))
        alpha = jnp.exp2(m_old - m_new)
        p = jnp.exp2((s_all - m_new).astype(BF16))
        acc_ref[...] = alpha * acc_ref[...] + _dot(ckc_t, p)
        m_ref[...] = m_new

    def far_pair(jp, carry):
        att_step(2 * jp, None, nchunks=2)
        return carry

    nfar = jnp.maximum(nch - 2, 0)
    lax.fori_loop(0, nfar >> 1, far_pair, 0)

    @pl.when((nfar & 1) == 1)
    def _():
        att_step(nfar - 1, None)

    @pl.when(nch >= 2)
    def _():
        att_step(nch - 2, d1_ref)

    att_step(nch - 1, d0_ref)

    o_ref[...] = (acc_ref[0:KV_RANK, :] / acc_ref[KV_RANK:KV_RANK + 1, :]).astype(BF16)


def _attention(qlat_w, qi_w, wi_w, ckv, ckv_t, ki, d0, d1, batch, seq):
    qb = Q_BLK
    nq = seq // qb
    n = batch * seq
    topk = min(TOPK_MAX, seq // 4)
    kern = functools.partial(_attn_kernel, topk=topk, seq=seq)
    wide = lambda h: pl.BlockSpec((h, N_HEADS * qb), lambda b, j: (0, b * nq + j))
    return pl.pallas_call(
        kern,
        grid=(batch, nq),
        in_specs=[
            wide(KV_RANK), wide(IDX_DIM), wide(1),
            pl.BlockSpec((seq, KV_RANK), lambda b, j: (b, 0)),
            pl.BlockSpec((CKVT_ROWS, seq), lambda b, j: (0, b)),
            pl.BlockSpec((seq, IDX_DIM), lambda b, j: (b, 0)),
            _const_spec(d0.shape), _const_spec(d1.shape),
        ],
        out_specs=wide(KV_RANK),
        out_shape=jax.ShapeDtypeStruct((KV_RANK, N_HEADS * n), BF16),
        scratch_shapes=[
            pltpu.VMEM((seq, qb), F32),
            pltpu.VMEM((1, N_HEADS * qb), F32),
            pltpu.VMEM((CKVT_ROWS, N_HEADS * qb), F32),
            pltpu.VMEM((qb, qb), BF16),
            pltpu.VMEM((1, qb), F32),
        ],
        compiler_params=pltpu.CompilerParams(
            dimension_semantics=("arbitrary", "arbitrary"), vmem_limit_bytes=VMEM_LIMIT),
        name="sparse_attention",
    )(qlat_w, qi_w, wi_w, ckv, ckv_t, ki, d0, d1)


def _mix_kernel(x_ref, ol_ref, u_ref, v_ref, ga_ref, gb_ref, wuv_ref, wpa_ref, ws_ref, bs_ref,
                wpb_ref, wout_ref, g2_ref, wr_ref, br_ref, x1_ref, h2_ref, comb_ref, goh_ref):
    tm = x_ref.shape[0]
    qb = Q_BLK
    o_at = jnp.concatenate(
        [jnp.concatenate(
            [_dot(wuv_ref[h], ol_ref[:, (blk * N_HEADS + h) * qb:(blk * N_HEADS + h + 1) * qb])
             for h in range(N_HEADS)], axis=0)
         for blk in range(tm // qb)], axis=1)
    y_a = _dot(jnp.transpose(o_at).astype(BF16), wpa_ref[...])
    v = v_ref[...]
    u = u_ref[...].astype(F32)
    rows = []
    for c in range(tm // CHUNK):
        cols = []
        for g in range(GMLP_GROUPS):
            vc = v[c * CHUNK:(c + 1) * CHUNK, g * GMLP_GROUP_DIM:(g + 1) * GMLP_GROUP_DIM]
            cols.append(_dot(ws_ref[g], vc) + bs_ref[g])
        rows.append(jnp.concatenate(cols, axis=1))
    s = jnp.concatenate(rows, axis=0)
    y_b = _dot((u * s).astype(BF16), wpb_ref[...])
    merged = ga_ref[...].astype(F32) * y_a + gb_ref[...].astype(F32) * y_b
    x1 = x_ref[...] + _dot(merged.astype(BF16), wout_ref[...])
    x1_ref[...] = x1
    h2 = _rms(x1, g2_ref[...])
    h2_hi = h2.astype(BF16)
    h2_ref[...] = h2_hi

    h2_lo = (h2 - h2_hi.astype(F32)).astype(BF16)
    logits = (_dot(h2_hi, wr_ref[0]) + (_dot(h2_lo, wr_ref[0]) + _dot(h2_hi, wr_ref[1]))
              + br_ref[...])
    lane = lax.broadcasted_iota(jnp.int32, logits.shape, 1)
    big = jnp.int32(4 * LANES)
    is_g = (lane >= N_EXPERTS) & (lane < N_EXPERTS + N_GROUPS)
    lg = jnp.where(is_g, logits, NEG)
    gmax = jnp.max(lg, axis=1, keepdims=True)
    g_w = 1.0 / jnp.sum(jnp.where(is_g, jnp.exp(lg - gmax), 0.0), axis=1, keepdims=True)
    g_idx = jnp.min(jnp.where(is_g & (lg == gmax), lane, big), axis=1, keepdims=True) - N_EXPERTS
    in_g = (lane < N_EXPERTS) & ((lane >> 3) == g_idx)
    le = jnp.where(in_g, logits, NEG)
    emax = jnp.max(le, axis=1, keepdims=True)
    ee = jnp.where(in_g, jnp.exp(le - emax), 0.0)
    ep = ee / jnp.sum(ee, axis=1, keepdims=True)
    ep1 = jnp.where(in_g, ep, -1.0)
    m1 = jnp.max(ep1, axis=1, keepdims=True)
    i1 = jnp.min(jnp.where(ep1 == m1, lane, big), axis=1, keepdims=True)
    ep2 = jnp.where(lane == i1, -1.0, ep1)
    m2 = jnp.max(ep2, axis=1, keepdims=True)
    i2 = jnp.min(jnp.where(ep2 == m2, lane, big), axis=1, keepdims=True)
    tot = m1 + m2
    comb_ref[...] = (jnp.where(lane == i1, g_w * m1 / tot, 0.0)
                     + jnp.where(lane == i2, g_w * m2 / tot, 0.0))
    goh_ref[...] = jnp.transpose(jnp.where(lane == g_idx, 1.0, 0.0))[0:SUBLANES, :]


def _mix(x2, olat_t, u, v, ga, gb, wuv, wpa, ws, bs, wpb, wout, g2, wr, br):
    n, d = x2.shape
    tm = TM_MIX
    row = lambda w: pl.BlockSpec((tm, w), lambda i: (i, 0))
    weights = (wuv, wpa, ws, bs, wpb, wout, g2, wr, br)
    return pl.pallas_call(
        _mix_kernel,
        grid=(n // tm,),
        in_specs=[row(d), pl.BlockSpec((KV_RANK, N_HEADS * tm), lambda i: (0, i)),
                  row(GMLP_WIDTH), row(GMLP_WIDTH), row(d), row(d)]
        + [_const_spec(w.shape) for w in weights],
        out_specs=(row(d), row(d), row(LANES), pl.BlockSpec((SUBLANES, tm), lambda i: (0, i))),
        out_shape=(jax.ShapeDtypeStruct((n, d), F32), jax.ShapeDtypeStruct((n, d), BF16),
                   jax.ShapeDtypeStruct((n, LANES), F32),
                   jax.ShapeDtypeStruct((SUBLANES, n), F32)),
        compiler_params=pltpu.CompilerParams(
            dimension_semantics=("arbitrary",), vmem_limit_bytes=VMEM_LIMIT),
        name="merge_router",
    )(x2, olat_t, u, v, ga, gb, *weights)


def _moe_kernel(x1_ref, h2_ref, comb_ref, goh_ref, tri_ref, wg_ref, wu_ref, wd_ref, gf_ref, o_ref,
                perm_ref, pos_ref, hs_ref, cs_ref, ys_ref, seg_ref):
    step = pl.program_id(1)
    tt, d = hs_ref.shape
    rb = MOE_ROWS
    pc = MOE_PERM_ROWS

    @pl.when(step == 0)
    def _():
        goh = goh_ref[...]
        before = _dot(goh.astype(BF16), tri_ref[...])
        pos = jnp.zeros((1, tt), F32)
        start = jnp.float32(0.0)
        for g in range(N_GROUPS):
            cnt = jnp.sum(goh[g:g + 1, :])
            seg_ref[g] = jnp.floor(start * (1.0 / rb)).astype(jnp.int32)
            seg_ref[N_GROUPS + g] = jnp.ceil((start + cnt) * (1.0 / rb)).astype(jnp.int32)
            pos = pos + goh[g:g + 1, :] * (start + before[g:g + 1, :])
            start = start + cnt
        pos_ref[...] = pos
        for r in range(tt // pc):
            dst = (lax.broadcasted_iota(jnp.int32, (pc, tt), 0) + r * pc).astype(F32)
            perm_ref[r * pc:(r + 1) * pc, :] = jnp.where(pos == dst, 1.0, 0.0).astype(BF16)
        perm = perm_ref[...]
        hs_ref[...] = _dot(perm, h2_ref[...]).astype(BF16)
        comb = comb_ref[...]
        comb_hi = comb.astype(BF16)
        comb_lo = (comb - comb_hi.astype(F32)).astype(BF16)
        cs_ref[...] = _dot(perm, comb_hi) + _dot(perm, comb_lo)
        ys_ref[...] = jnp.zeros(ys_ref.shape, F32)

    group = step // (EXPERTS_PER_GROUP // MOE_EXPERTS_PER_STEP)
    first = seg_ref[group]
    last = seg_ref[N_GROUPS + group]

    def block(b, carry):
        rows = pl.ds(pl.multiple_of(b * rb, rb), rb)
        hb = hs_ref[rows, :]
        cb = cs_ref[rows, :]
        lane = lax.broadcasted_iota(jnp.int32, cb.shape, 1)
        y = jnp.zeros((rb, d), F32)
        for k in range(MOE_EXPERTS_PER_STEP):
            e = step * MOE_EXPERTS_PER_STEP + k
            w_e = jnp.sum(jnp.where(lane == e, cb, 0.0), axis=1, keepdims=True)
            act = jax.nn.silu(_dot(hb, wg_ref[k])) * _dot(hb, wu_ref[k]) * w_e
            y = y + _dot(act.astype(BF16), wd_ref[k])
        ys_ref[rows, :] += y
        return carry

    lax.fori_loop(first, last, block, 0)

    @pl.when(step == pl.num_programs(1) - 1)
    def _():
        ys = ys_ref[...].astype(BF16)
        pos_col = jnp.transpose(jnp.broadcast_to(pos_ref[...], (SUBLANES, tt)))[:, 0:1]
        for r in range(tt // pc):
            rows = slice(r * pc, (r + 1) * pc)
            src = lax.broadcasted_iota(jnp.int32, (pc, tt), 1).astype(F32)
            unperm = jnp.where(pos_col[rows] == src, 1.0, 0.0).astype(BF16)
            o_ref[rows, :] = _rms(x1_ref[rows, :] + _dot(unperm, ys), gf_ref[...])


def _moe(x1, h2, comb, goh_t, wg, wu, wd, gf):
    n, d = x1.shape
    tt = min(TM_MOE, n)
    eps = MOE_EXPERTS_PER_STEP
    assert EXPERTS_PER_GROUP % eps == 0 and tt % MOE_ROWS == 0 and tt % MOE_PERM_ROWS == 0
    row = lambda w: pl.BlockSpec((tt, w), lambda i, s: (i, 0))
    tri = (jnp.arange(tt)[:, None] < jnp.arange(tt)[None, :]).astype(BF16)
    return pl.pallas_call(
        _moe_kernel,
        grid=(n // tt, N_EXPERTS // eps),
        in_specs=[row(d), row(d), row(LANES),
                  pl.BlockSpec((SUBLANES, tt), lambda i, s: (0, i)),
                  _const_spec(tri.shape),
                  pl.BlockSpec((eps, d, D_FF_EXPERT), lambda i, s: (s, 0, 0)),
                  pl.BlockSpec((eps, d, D_FF_EXPERT), lambda i, s: (s, 0, 0)),
                  pl.BlockSpec((eps, D_FF_EXPERT, d), lambda i, s: (s, 0, 0)),
                  _const_spec(gf.shape)],
        out_specs=row(d),
        out_shape=jax.ShapeDtypeStruct((n, d), F32),
        scratch_shapes=[
            pltpu.VMEM((tt, tt), BF16),
            pltpu.VMEM((1, tt), F32),
            pltpu.VMEM((tt, d), BF16),
            pltpu.VMEM((tt, LANES), F32),
            pltpu.VMEM((tt, d), F32),
            pltpu.SMEM((2 * N_GROUPS,), jnp.int32),
        ],
        compiler_params=pltpu.CompilerParams(
            dimension_semantics=("arbitrary", "arbitrary"), vmem_limit_bytes=VMEM_LIMIT),
        name="experts",
    )(x1, h2, comb, goh_t, tri, wg, wu, wd, gf)


def _t5_bucket(n):
    max_exact = NUM_BUCKETS // 2
    nf = jnp.maximum(n, 1).astype(F32)
    large = max_exact + jnp.floor(
        jnp.log(nf / max_exact) / np.float32(np.log(MAX_DISTANCE / max_exact))
        * (NUM_BUCKETS - max_exact)).astype(jnp.int32)
    large = jnp.minimum(large, NUM_BUCKETS - 1)
    return jnp.where(n < max_exact, n, large)


def _bias_tiles(rel_bias):
    qb = Q_BLK
    k = jnp.arange(qb, dtype=jnp.int32)[:, None]
    q = jnp.arange(qb, dtype=jnp.int32)[None, :]
    rel = (rel_bias - rel_bias[NUM_BUCKETS - 1]) * LOG2E

    def tile(dist):
        onehot = (_t5_bucket(dist)[:, :, None] == jnp.arange(NUM_BUCKETS)).astype(F32)
        t = jnp.einsum("kqb,bh->khq", onehot, rel, precision=lax.Precision.HIGHEST)
        return t.reshape(qb, N_HEADS * qb)

    return tile(jnp.maximum(q - k, 0)), tile(qb + q - k)


def _block_diag(w):
    h, a, b = w.shape
    eye = jnp.eye(h, dtype=w.dtype)
    return (eye[:, None, :, None] * w[:, :, None, :]).reshape(h * a, h * b)


def kernel(x, w_in, kv_norm_g, w_uk, w_uv, rel_bias, ln_v_g, ln_v_b, w_spatial, b_spatial,
           w_proj_a, w_proj_b, w_out, norm1_g, norm2_g, router_group_w, router_group_b,
           router_expert_w, router_expert_b, w_gate, w_up, w_down, final_norm_g):
    batch, seq, d = x.shape
    depth = w_in.shape[0]
    n = batch * seq
    assert depth == 1, "the final rms-norm is fused into the (single) layer's expert kernel"
    assert EXPERTS_PER_GROUP == 8
    assert seq % (2 * Q_BLK) == 0 and n % min(TM_MOE, n) == 0
    assert TM_PROJ % Q_BLK == 0 and n % TM_PROJ == 0 and TM_MIX % Q_BLK == 0 and n % TM_MIX == 0
    assert Q_BLK + 1 >= MAX_DISTANCE

    attn_w = N_HEADS * HEAD_DIM
    sizes = (attn_w, KV_RANK, IDX_HEADS * IDX_DIM, IDX_DIM, IDX_HEADS, GMLP_WIDTH, GMLP_WIDTH, d, d)
    offs = np.concatenate([[0], np.cumsum(sizes)])
    d0, d1 = _bias_tiles(rel_bias)
    tril = jnp.tril(jnp.ones((CHUNK, CHUNK), dtype=bool))
    row = lambda a: a.reshape(1, -1)

    x2 = x.reshape(n, d)
    l = 0
    wl = w_in[l].astype(BF16)
    wq, wc, wqi, wki, wwi, wu, wv, wga, wgb = [wl[:, offs[i]:offs[i + 1]] for i in range(9)]
    wt = jnp.concatenate([wq, wqi, wc, wwi], axis=1).T
    wuk_bd = _block_diag(jnp.transpose(w_uk[l], (1, 0, 2))).astype(BF16)
    wuv_t = jnp.transpose(w_uv[l], (1, 2, 0)).astype(BF16)

    qlat_t, qi_t, wi_t, ckv_t, ckv, ki, u, v, ga, gb = _projection(
        x2, row(norm1_g[l]), wt, wuk_bd, kv_norm_g[l].reshape(-1, 1), wc, row(kv_norm_g[l]),
        wki, wu, wv, row(ln_v_g[l]), row(ln_v_b[l]), wga, wgb)

    olat_t = _attention(qlat_t, qi_t, wi_t, ckv, ckv_t, ki, d0, d1, batch, seq)

    ws = jnp.where(tril[None], w_spatial[l], 0.0).astype(BF16)
    bs = jnp.broadcast_to(b_spatial[l][:, :, None], (GMLP_GROUPS, CHUNK, GMLP_GROUP_DIM))
    wr = jnp.concatenate(
        [router_expert_w[l], router_group_w[l],
         jnp.zeros((d, LANES - N_EXPERTS - N_GROUPS), F32)], axis=1)
    wr_hi = wr.astype(BF16)
    wr = jnp.stack([wr_hi, (wr - wr_hi.astype(F32)).astype(BF16)])
    br = jnp.concatenate(
        [router_expert_b[l], router_group_b[l],
         jnp.zeros((LANES - N_EXPERTS - N_GROUPS,), F32)]).reshape(1, LANES)
    x1, h2, comb, goh_t = _mix(
        x2, olat_t, u, v, ga, gb, wuv_t, w_proj_a[l].astype(BF16), ws, bs,
        w_proj_b[l].astype(BF16), w_out[l].astype(BF16), row(norm2_g[l]), wr, br)

    out = _moe(x1, h2, comb, goh_t, w_gate[l].astype(BF16), w_up[l].astype(BF16),
               w_down[l].astype(BF16), row(final_norm_g))
    return out.reshape(batch, seq, d)
```
